```python
import math
import jax
import jax.numpy as jnp
from jax import lax
import numpy as np

D_MODEL = 1024
BATCH = 4
SEQ = 4096
DEPTH = 4

CTX_LEN = 256
GRID_W = 64
EPS = 1e-6

A_HEADS = 4
A_QK = 128
A_V = 256
A_WIDTH = A_HEADS * A_V
A_CHUNK = 64
F_BIAS_LO = 3.0
F_BIAS_HI = 6.0

B_HEADS = 8
B_QK = 64
B_V = 128
B_WIDTH = B_HEADS * B_V
B_QBLOCK = 128
ROPE_AXIS_DIM = B_QK // 2
ROPE_THETA = 10000.0

C_GROUPS = 4
C_GROUP_DIM = 256
C_WIDTH = C_GROUPS * C_GROUP_DIM

N_BRANCH = 3

COL_SIZES = (
    A_HEADS * A_QK, A_HEADS * A_QK, A_WIDTH, 4 * A_HEADS, A_WIDTH, A_WIDTH,
    B_HEADS * 2 * B_QK, B_HEADS * 2 * B_QK, B_WIDTH, B_WIDTH,
    C_WIDTH, C_WIDTH,
    N_BRANCH * D_MODEL,
)
D_IN = 2 * A_HEADS * A_QK + 3 * A_WIDTH + 4 * A_HEADS + 4 * B_HEADS * B_QK + 2 * B_WIDTH + 2 * C_WIDTH + N_BRANCH * D_MODEL

kernel_name = "hybrid_mlstm_diffattn_fourier_prefix_block"


def rmsnorm(x, w):
    xf = x.astype(jnp.float32)
    y = xf * lax.rsqrt(jnp.mean(xf * xf, axis=-1, keepdims=True) + EPS)
    return (y * w.astype(jnp.float32)).astype(x.dtype)


def split_cols(p):
    parts, start = [], 0
    for size in COL_SIZES:
        parts.append(p[..., start:start + size])
        start += size
    return parts


def axial_rope_tables(n_tokens):
    rows = n_tokens // GRID_W
    row = jnp.repeat(jnp.arange(rows, dtype=jnp.float32), GRID_W)
    col = jnp.tile(jnp.arange(GRID_W, dtype=jnp.float32), rows)
    inv_freq = ROPE_THETA ** (-jnp.arange(0, ROPE_AXIS_DIM, 2, dtype=jnp.float32) / ROPE_AXIS_DIM)
    ang_r = row[:, None] * inv_freq
    ang_c = col[:, None] * inv_freq
    ex = lambda t: t[:, None, None, :]
    return (ex(jnp.cos(ang_r)), ex(jnp.sin(ang_r)), ex(jnp.cos(ang_c)), ex(jnp.sin(ang_c)))


def _rotate(x, cos, sin):
    x1, x2 = jnp.split(x, 2, axis=-1)
    return jnp.concatenate([x1 * cos - x2 * sin, x2 * cos + x1 * sin], axis=-1)


def apply_axial_rope(x, rope):
    cos_r, sin_r, cos_c, sin_c = rope
    xr, xc = jnp.split(x.astype(jnp.float32), 2, axis=-1)
    return jnp.concatenate([_rotate(xr, cos_r, sin_r), _rotate(xc, cos_c, sin_c)], axis=-1).astype(x.dtype)


def mlstm_prep(q, k, v, g, b_if):
    bsz, n, _ = q.shape
    heads = lambda t, d: t.astype(jnp.float32).reshape(bsz, n, A_HEADS, d).transpose(0, 2, 1, 3)
    qh = heads(q, A_QK)
    kh = heads(k, A_QK) * (A_QK ** -0.5)
    vh = heads(v, A_V)
    pre = (g.astype(jnp.float32).reshape(bsz, n, 4, A_HEADS)
           + b_if.astype(jnp.float32).reshape(4, A_HEADS)).transpose(2, 0, 3, 1)
    fwd = (pre[0], jax.nn.log_sigmoid(pre[1]))
    bwd = (pre[2], jax.nn.log_sigmoid(pre[3]))
    return qh, kh, vh, (fwd, bwd)


def mlstm_zero_state(bsz):
    return (jnp.zeros((bsz, A_HEADS, A_QK, A_V), jnp.float32),
            jnp.zeros((bsz, A_HEADS, A_QK), jnp.float32),
            jnp.zeros((bsz, A_HEADS), jnp.float32))


def mlstm_scan(q, k, v, ig, lf, state):
    bsz, nh, n, _ = q.shape
    nc = n // A_CHUNK

    def chunks(t):
        return jnp.moveaxis(t.reshape(bsz, nh, nc, A_CHUNK, *t.shape[3:]), 2, 0)

    tril = jnp.tril(jnp.ones((A_CHUNK, A_CHUNK), dtype=bool))

    def body(carry, inp):
        c_st, n_st, m_st = carry
        qc, kc, vc, ic, fc = inp
        b = jnp.cumsum(fc, axis=-1)
        dmat = jnp.where(tril, b[..., :, None] - b[..., None, :] + ic[..., None, :], -jnp.inf)
        inter = b + m_st[..., None]
        m_row = jnp.maximum(jnp.max(dmat, axis=-1), inter)
        a = jnp.exp(dmat - m_row[..., None]) * jnp.einsum('bhtd,bhsd->bhts', qc, kc)
        w_inter = jnp.exp(inter - m_row)
        num = jnp.einsum('bhts,bhsv->bhtv', a, vc) + w_inter[..., None] * jnp.einsum('bhtd,bhdv->bhtv', qc, c_st)
        den = jnp.sum(a, axis=-1) + w_inter * jnp.einsum('bhtd,bhd->bht', qc, n_st)
        h = num / jnp.maximum(jnp.abs(den), jnp.exp(-m_row))[..., None]
        b_last = b[..., -1]
        g = b_last[..., None] - b + ic
        m_new = jnp.maximum(b_last + m_st, jnp.max(g, axis=-1))
        wk = jnp.exp(g - m_new[..., None])
        decay = jnp.exp(b_last + m_st - m_new)
        c_new = decay[..., None, None] * c_st + jnp.einsum('bhsd,bhsv->bhdv', kc * wk[..., None], vc)
        n_new = decay[..., None] * n_st + jnp.einsum('bhs,bhsd->bhd', wk, kc)
        return (c_new, n_new, m_new), h

    state, h = lax.scan(body, state, (chunks(q), chunks(k), chunks(v), chunks(ig), chunks(lf)))
    return jnp.moveaxis(h, 0, 2).reshape(bsz, nh, n, v.shape[-1]), state


def mlstm_direction(q, k, v, ig, lf, state, reverse):
    if reverse:
        h, state = mlstm_scan(jnp.flip(q, 2), jnp.flip(k, 2), jnp.flip(v, 2), jnp.flip(ig, 2), jnp.flip(lf, 2), state)
        return jnp.flip(h, 2), state
    return mlstm_scan(q, k, v, ig, lf, state)


def mlstm_mixer(q, k, v, gates, q_c, k_c, v_c, gates_c):
    h_lat, h_ctx = [], []
    for d, reverse in enumerate((False, True)):
        ig_c, lf_c = gates_c[d]
        ig, lf = gates[d]
        hc_d, st = mlstm_direction(q_c, k_c, v_c, ig_c, lf_c, mlstm_zero_state(q_c.shape[0]), reverse)
        h_d, _ = mlstm_direction(q, k, v, ig, lf, st, reverse)
        h_lat.append(h_d)
        h_ctx.append(hc_d)
    return h_lat[0] + h_lat[1], h_ctx[0] + h_ctx[1]


def mlstm_out(h, o, z, a_norm_w):
    bsz, _, n, _ = h.shape
    hn = rmsnorm(h.transpose(0, 2, 1, 3), a_norm_w.reshape(A_HEADS, A_V)).reshape(bsz, n, A_WIDTH)
    return hn.astype(o.dtype) * jax.nn.sigmoid(o) * jax.nn.silu(z)


def diffattn_prep(q, k, v, q_norm_w, k_norm_w, rope):
    bsz, n, _ = q.shape
    qh = rmsnorm(q.reshape(bsz, n, B_HEADS, 2, B_QK), q_norm_w)
    kh = rmsnorm(k.reshape(bsz, n, B_HEADS, 2, B_QK), k_norm_w)
    if rope is not None:
        qh = apply_axial_rope(qh, rope)
        kh = apply_axial_rope(kh, rope)
    vh = v.reshape(bsz, n, B_HEADS, B_V).transpose(0, 2, 1, 3)
    return qh.transpose(0, 2, 3, 1, 4), kh.transpose(0, 2, 3, 1, 4), vh


def attend(q, k, v, lam):
    s = jnp.einsum('bhmqd,bhmkd->bhmqk', q, k, preferred_element_type=jnp.float32) * (B_QK ** -0.5)
    p = jax.nn.softmax(s, axis=-1)
    w = p[:, :, 0] - lam * p[:, :, 1]
    return jnp.einsum('bhqk,bhkv->bhqv', w.astype(v.dtype), v)


def attend_blocked(q, k, v, lam):
    bsz, nh, nm, n, d = q.shape
    nb = n // B_QBLOCK
    qb = jnp.moveaxis(q.reshape(bsz, nh, nm, nb, B_QBLOCK, d), 3, 0)
    o = lax.map(lambda blk: attend(blk, k, v, lam), qb)
    return jnp.moveaxis(o, 0, 2).reshape(bsz, nh, n, v.shape[-1])


def diffattn_out(o, subln_w, lam_init):
    bsz, _, n, _ = o.shape
    return (rmsnorm(o, subln_w) * (1.0 - lam_init)).transpose(0, 2, 1, 3).reshape(bsz, n, B_WIDTH)


def fourier_mix(u):
    bsz, n, _ = u.shape
    uf = u.astype(jnp.float32).reshape(bsz, n, C_GROUPS, C_GROUP_DIM)
    y = jnp.fft.fft2(uf, axes=(1, 3), norm="ortho").real
    return y.reshape(bsz, n, C_WIDTH).astype(u.dtype)


def merge(y_a, y_b, y_c, mg, w_a_out, w_b_out, w_c_out, w_out):
    g_a, g_b, g_c = jnp.split(jax.nn.sigmoid(mg), N_BRANCH, axis=-1)
    y = g_a * (y_a @ w_a_out) + g_b * (y_b @ w_b_out) + g_c * (y_c @ w_c_out)
    return y @ w_out


def hybrid_layer(x, ctx, sc, sc_ctx, rope, layer_idx, last, norm_w, w_ada, b_ada, w_in, b_if, a_norm_w,
                 q_norm_w, k_norm_w, lambda_q1, lambda_k1, lambda_q2, lambda_k2, subln_w,
                 w_a_out, w_b_out, w_c_out, w_out):
    shift, scale, gate = jnp.split((sc @ w_ada + b_ada)[:, None, :], 3, axis=-1)
    shift_c, scale_c, gate_c = jnp.split(sc_ctx @ w_ada + b_ada, 3)
    h = rmsnorm(x, norm_w) * (1.0 + scale) + shift
    hc = rmsnorm(ctx, norm_w) * (1.0 + scale_c) + shift_c
    aq, ak, av, ag, ao, az, bq, bk, bv, bz, cu, cz, mg = split_cols(h @ w_in)
    aq_c, ak_c, av_c, ag_c, ao_c, az_c, bq_c, bk_c, bv_c, bz_c, cu_c, cz_c, mg_c = split_cols(hc @ w_in)

    qa, ka, va, ga = mlstm_prep(aq, ak, av, ag, b_if)
    qa_c, ka_c, va_c, ga_c = mlstm_prep(aq_c, ak_c, av_c, ag_c, b_if)
    h_a, h_a_c = mlstm_mixer(qa, ka, va, ga, qa_c, ka_c, va_c, ga_c)
    y_a = mlstm_out(h_a, ao, az, a_norm_w)

    lam_init = 0.8 - 0.6 * math.exp(-0.3 * layer_idx)
    f32 = jnp.float32
    lam = (jnp.exp(jnp.sum(lambda_q1.astype(f32) * lambda_k1.astype(f32)))
           - jnp.exp(jnp.sum(lambda_q2.astype(f32) * lambda_k2.astype(f32))) + lam_init)
    qb, kb, vb = diffattn_prep(bq, bk, bv, q_norm_w, k_norm_w, rope)
    qb_c, kb_c, vb_c = diffattn_prep(bq_c, bk_c, bv_c, q_norm_w, k_norm_w, None)
    o_b = attend_blocked(qb, jnp.concatenate([kb_c, kb], axis=3), jnp.concatenate([vb_c, vb], axis=2), lam)
    y_b = diffattn_out(o_b, subln_w, lam_init) * jax.nn.silu(bz)

    y_c = fourier_mix(cu) * jax.nn.silu(cz)

    x = x + gate * merge(y_a, y_b, y_c, mg, w_a_out, w_b_out, w_c_out, w_out)
    if not last:
        y_a_c = mlstm_out(h_a_c, ao_c, az_c, a_norm_w)
        y_b_c = diffattn_out(attend(qb_c, kb_c, vb_c, lam), subln_w, lam_init) * jax.nn.silu(bz_c)
        y_c_c = fourier_mix(cu_c) * jax.nn.silu(cz_c)
        ctx = ctx + gate_c * merge(y_a_c, y_b_c, y_c_c, mg_c, w_a_out, w_b_out, w_c_out, w_out)
    return x, ctx


def setup_inputs(seed: int = 0) -> dict:
    key = jax.random.key(seed)
    ks = jax.random.split(key, 24)
    f32 = jnp.float32
    nrm = lambda k, shape, s: jax.random.normal(k, shape, f32) * s
    f_base = jnp.linspace(F_BIAS_LO, F_BIAS_HI, A_HEADS, dtype=f32)
    zeros_h = jnp.zeros((A_HEADS,), f32)
    b_if = jnp.concatenate([zeros_h, f_base, zeros_h, f_base])[None, :] + nrm(ks[8], (DEPTH, 4 * A_HEADS), 0.1)
    return {
        "x": nrm(ks[0], (BATCH, SEQ, D_MODEL), 1.0),
        "c": nrm(ks[1], (BATCH, D_MODEL), 1.0),
        "ctx": nrm(ks[2], (BATCH, CTX_LEN, D_MODEL), 1.0),
        "c_ctx": nrm(ks[3], (D_MODEL,), 1.0),
        "norm_w": 1.0 + nrm(ks[4], (DEPTH, D_MODEL), 0.02),
        "w_ada": nrm(ks[5], (DEPTH, D_MODEL, 3 * D_MODEL), 0.5 * D_MODEL ** -0.5),
        "b_ada": nrm(ks[6], (DEPTH, 3 * D_MODEL), 0.02),
        "w_in": nrm(ks[7], (DEPTH, D_MODEL, D_IN), D_MODEL ** -0.5),
        "b_if": b_if,
        "a_norm_w": 1.0 + nrm(ks[9], (DEPTH, A_WIDTH), 0.02),
        "q_norm_w": 1.0 + nrm(ks[10], (DEPTH, B_QK), 0.02),
        "k_norm_w": 1.0 + nrm(ks[11], (DEPTH, B_QK), 0.02),
        "lambda_q1": nrm(ks[12], (DEPTH, B_QK), 0.1),
        "lambda_k1": nrm(ks[13], (DEPTH, B_QK), 0.1),
        "lambda_q2": nrm(ks[14], (DEPTH, B_QK), 0.1),
        "lambda_k2": nrm(ks[15], (DEPTH, B_QK), 0.1),
        "subln_w": 1.0 + nrm(ks[16], (DEPTH, B_V), 0.02),
        "w_a_out": nrm(ks[17], (DEPTH, A_WIDTH, D_MODEL), A_WIDTH ** -0.5),
        "w_b_out": nrm(ks[18], (DEPTH, B_WIDTH, D_MODEL), B_WIDTH ** -0.5),
        "w_c_out": nrm(ks[19], (DEPTH, C_WIDTH, D_MODEL), C_WIDTH ** -0.5),
        "w_out": nrm(ks[20], (DEPTH, D_MODEL, D_MODEL), D_MODEL ** -0.5),
    }


def reference(x, c, ctx, c_ctx, norm_w, w_ada, b_ada, w_in, b_if, a_norm_w, q_norm_w, k_norm_w,
              lambda_q1, lambda_k1, lambda_q2, lambda_k2, subln_w, w_a_out, w_b_out, w_c_out, w_out):
    rope = axial_rope_tables(x.shape[1])
    sc = jax.nn.silu(c)
    sc_ctx = jax.nn.silu(c_ctx)
    for l in range(DEPTH):
        x, ctx = hybrid_layer(x, ctx, sc, sc_ctx, rope, l, l == DEPTH - 1,
                              norm_w[l], w_ada[l], b_ada[l], w_in[l], b_if[l], a_norm_w[l],
                              q_norm_w[l], k_norm_w[l], lambda_q1[l], lambda_k1[l], lambda_q2[l], lambda_k2[l],
                              subln_w[l], w_a_out[l], w_b_out[l], w_c_out[l], w_out[l])
    return x
```

```python
import functools
import math

import jax
import jax.numpy as jnp
import numpy as np
from jax import lax
from jax.experimental import pallas as pl
from jax.experimental.pallas import tpu as pltpu

F32 = jnp.float32
BF16 = jnp.bfloat16

D_MODEL = 1024
DEPTH = 4
N_LAT = 4096
N_CTX = 256
N_TOK = N_LAT + N_CTX
GRID_W = 64
EPS = 1e-6

A_HEADS, A_QK, A_V = 4, 128, 256
B_HEADS, B_QK, B_V = 8, 64, 128
C_GROUPS, C_GROUP_DIM = 4, 256
ROPE_AXIS_DIM = B_QK // 2
ROPE_THETA = 10000.0
N_GATE = 4 * A_HEADS
BATCH = 4
B_CTX_ROW = BATCH

COL_AQK, COL_AV, COL_AO, COL_AZ, COL_BQ, COL_BK, COL_BV, COL_BZ, COL_CU, COL_CZ, COL_MG = (
    0, 1, 2, 3, 4, 5, 6, 7, 8, 9, 10)
N_PROJ = 13 * D_MODEL
GATE_COL0 = 2 * A_HEADS * A_QK + A_HEADS * A_V

LANES = 128
SUBLANES = 8
VMEM_LIMIT_BYTES = 56 * 1024 * 1024

TOK_TILE = 544
MM_TM = 1024
MM_TN = 1024
ATT_TQ = 256
MLSTM_CHUNK = 256
DFT_TM = 1024
DFT_TK = 2048


def _cparams(*sem):
    return pltpu.CompilerParams(dimension_semantics=sem, vmem_limit_bytes=VMEM_LIMIT_BYTES)


def _split3(x):
    hi = x.astype(BF16)
    r1 = x - hi.astype(F32)
    mid = r1.astype(BF16)
    lo = (r1 - mid.astype(F32)).astype(BF16)
    return hi, mid, lo


def _dot(a, b):
    return jnp.dot(a, b, preferred_element_type=F32)


def _dot_f32_rhs(a_exact, b):
    hi, mid, lo = _split3(b)
    return _dot(a_exact, hi) + _dot(a_exact, mid) + _dot(a_exact, lo)


def _dot_f32_lhs(a, b_exact):
    hi, mid, lo = _split3(a)
    return _dot(hi, b_exact) + _dot(mid, b_exact) + _dot(lo, b_exact)


def _dot_f32(a, b):
    a_hi = a.astype(BF16)
    a_lo = (a - a_hi.astype(F32)).astype(BF16)
    b_hi = b.astype(BF16)
    b_lo = (b - b_hi.astype(F32)).astype(BF16)
    return _dot(a_hi, b_hi) + _dot(a_lo, b_hi) + _dot(a_hi, b_lo)


def _sigmoid(x):
    return 1.0 / (1.0 + jnp.exp(-x))


def _silu(x):
    return x * _sigmoid(x)


def _log_sigmoid(x):
    return jnp.minimum(x, 0.0) - jnp.log(1.0 + jnp.exp(-jnp.abs(x)))


def _mod_kernel(c_ref, w_ref, b_ref, q1_ref, k1_ref, q2_ref, k2_ref, mod_ref, lam_ref):
    sc = _silu(c_ref[...])
    mod_ref[0] = _dot_f32(sc, w_ref[0]) + b_ref[0]
    layer = jnp.full((1, LANES), pl.program_id(0), dtype=jnp.int32).astype(F32)
    lam_init = 0.8 - 0.6 * jnp.exp(-0.3 * layer)
    s1 = jnp.sum(q1_ref[0] * k1_ref[0], axis=-1, keepdims=True)
    s2 = jnp.sum(q2_ref[0] * k2_ref[0], axis=-1, keepdims=True)
    lam_ref[0] = jnp.exp(s1) - jnp.exp(s2) + lam_init


def _modulation(c_rows, w_ada, b_ada, lq1, lk1, lq2, lk2):
    n_col = 3
    vec = lambda a: a.reshape(DEPTH, 1, B_QK)
    lam_spec = pl.BlockSpec((1, 1, B_QK), lambda l, j: (l, 0, 0))
    return pl.pallas_call(
        _mod_kernel,
        grid=(DEPTH, n_col),
        in_specs=[
            pl.BlockSpec((SUBLANES, D_MODEL), lambda l, j: (0, 0)),
            pl.BlockSpec((1, D_MODEL, D_MODEL), lambda l, j: (l, 0, j)),
            pl.BlockSpec((1, 1, D_MODEL), lambda l, j: (l, 0, j)),
            lam_spec, lam_spec, lam_spec, lam_spec,
        ],
        out_specs=[
            pl.BlockSpec((1, SUBLANES, D_MODEL), lambda l, j: (l, 0, j)),
            pl.BlockSpec((1, 1, LANES), lambda l, j: (l, 0, 0)),
        ],
        out_shape=[
            jax.ShapeDtypeStruct((DEPTH, SUBLANES, 3 * D_MODEL), F32),
            jax.ShapeDtypeStruct((DEPTH, 1, LANES), F32),
        ],
        compiler_params=_cparams("arbitrary", "arbitrary"),
        name="adaln_modulation",
    )(c_rows, w_ada, b_ada.reshape(DEPTH, 1, 3 * D_MODEL), vec(lq1), vec(lk1), vec(lq2), vec(lk2))


def _ctx_rows(tile_idx, tile):
    row = tile_idx * tile + lax.broadcasted_iota(jnp.int32, (tile, 1), 0)
    return row >= N_LAT


def _norm_mod_kernel(x_ref, nw_ref, mod_ref, wg_ref, bif_ref, h_ref, g_ref):
    b = pl.program_id(0)
    x = x_ref[0]
    y = x * lax.rsqrt(jnp.mean(x * x, axis=-1, keepdims=True) + EPS) * nw_ref[...]
    is_ctx = _ctx_rows(pl.program_id(1), TOK_TILE)
    m_lat = mod_ref[pl.ds(b, 1), :]
    m_ctx = mod_ref[pl.ds(B_CTX_ROW, 1), :]
    shift = jnp.where(is_ctx, m_ctx[:, 0:D_MODEL], m_lat[:, 0:D_MODEL])
    scale = jnp.where(is_ctx, m_ctx[:, D_MODEL:2 * D_MODEL], m_lat[:, D_MODEL:2 * D_MODEL])
    h = y * (1.0 + scale) + shift
    h_ref[0] = h.astype(BF16)
    g_ref[0] = _dot_f32(h, wg_ref[...]) + bif_ref[...]


def _norm_mod(xs, norm_w, mod_l, w_gate, b_if):
    bsz = xs.shape[0]
    tok = pl.BlockSpec((1, TOK_TILE, D_MODEL), lambda b, t: (b, t, 0))
    full = lambda shape: pl.BlockSpec(shape, lambda b, t: (0,) * len(shape))
    return pl.pallas_call(
        _norm_mod_kernel,
        grid=(bsz, N_TOK // TOK_TILE),
        in_specs=[tok, full((1, D_MODEL)), full((SUBLANES, 3 * D_MODEL)),
                  full((D_MODEL, LANES)), full((1, LANES))],
        out_specs=[tok, pl.BlockSpec((1, TOK_TILE, LANES), lambda b, t: (b, t, 0))],
        out_shape=[jax.ShapeDtypeStruct((bsz, N_TOK, D_MODEL), BF16),
                   jax.ShapeDtypeStruct((bsz, N_TOK, LANES), F32)],
        compiler_params=_cparams("parallel", "parallel"),
        name="norm_modulate",
    )(xs, norm_w, mod_l, w_gate, b_if)


def _matmul_kernel(a_ref, w_ref, o_ref):
    o_ref[...] = _dot(a_ref[...], w_ref[...]).astype(o_ref.dtype)


def _in_projection(h2d, w):
    m, k = h2d.shape
    n = w.shape[1]
    return pl.pallas_call(
        _matmul_kernel,
        grid=(m // MM_TM, n // MM_TN),
        in_specs=[pl.BlockSpec((MM_TM, k), lambda i, j: (i, 0)),
                  pl.BlockSpec((k, MM_TN), lambda i, j: (0, j))],
        out_specs=pl.BlockSpec((MM_TM, MM_TN), lambda i, j: (i, j)),
        out_shape=jax.ShapeDtypeStruct((m, n), BF16),
        compiler_params=_cparams("parallel", "arbitrary"),
        name="in_projection",
    )(h2d, w)


def _qk_prep_kernel(q_ref, k_ref, cos_ref, sin_ref, qw_ref, kw_ref, g_ref, qo_ref, ko_ref):
    cos = cos_ref[...]
    sin = sin_ref[...]
    group_sum = g_ref[...]
    lane = lax.broadcasted_iota(jnp.int32, (1, LANES), 1)
    first_half = (lane % ROPE_AXIS_DIM) < (ROPE_AXIS_DIM // 2)

    def prep(x_bf16, w, out_scale):
        x = x_bf16.astype(F32)
        ms = _dot_f32_lhs(x * x, group_sum) * (1.0 / B_QK)
        y = x * lax.rsqrt(ms + EPS) * w
        half = ROPE_AXIS_DIM // 2
        partner = jnp.where(first_half, pltpu.roll(y, LANES - half, 1), pltpu.roll(y, half, 1))
        return ((y * cos + partner * sin) * out_scale).astype(BF16)

    for s in range(D_MODEL // LANES):
        cols = slice(s * LANES, (s + 1) * LANES)
        qo_ref[0, :, cols] = prep(q_ref[0, :, cols], qw_ref[...], B_QK ** -0.5)
        ko_ref[0, :, cols] = prep(k_ref[0, :, cols], kw_ref[...], 1.0)


def _qk_prep(proj, cos_t, sin_t, qw, kw, group_sum):
    bsz = proj.shape[0]
    col = lambda c: pl.BlockSpec((1, TOK_TILE, D_MODEL), lambda b, t: (b, t, c))
    tab = pl.BlockSpec((TOK_TILE, LANES), lambda b, t: (t, 0))
    full = lambda shape: pl.BlockSpec(shape, lambda b, t: (0,) * len(shape))
    out = pl.BlockSpec((1, TOK_TILE, D_MODEL), lambda b, t: (b, t, 0))
    return pl.pallas_call(
        _qk_prep_kernel,
        grid=(bsz, N_TOK // TOK_TILE),
        in_specs=[col(COL_BQ), col(COL_BK), tab, tab, full((1, LANES)), full((1, LANES)),
                  full((LANES, LANES))],
        out_specs=[out, out],
        out_shape=[jax.ShapeDtypeStruct((bsz, N_TOK, D_MODEL), BF16)] * 2,
        compiler_params=_cparams("parallel", "parallel"),
        name="qk_norm_rope",
    )(proj, proj, cos_t, sin_t, qw, kw, group_sum)


def _diff_attn_kernel(lam_ref, q_ref, k_ref, v_ref, z_ref, sw_ref, o_ref, *, lam_init):
    qi = pl.program_id(2)
    lam = lam_ref[0]
    q = q_ref[0]
    lane = lax.broadcasted_iota(jnp.int32, q.shape, 1)
    zero = jnp.zeros_like(q)
    q_map0 = jnp.where(lane < B_QK, q, zero)
    q_map1 = jnp.where(lane >= B_QK, q, zero)
    nt = (((1,), (1,)), ((), ()))

    def softmax_terms(qm, keys):
        s = lax.dot_general(qm, keys, nt, preferred_element_type=F32)
        p = jnp.exp(s - jnp.max(s, axis=-1, keepdims=True))
        return p, jnp.sum(p, axis=-1, keepdims=True)

    def attend(keys, vals):
        p0, l0 = softmax_terms(q_map0, keys)
        p1, l1 = softmax_terms(q_map1, keys)
        w = (p0 - (lam * l0 / l1) * p1).astype(BF16)
        o = _dot(w, vals) / l0
        y = o * lax.rsqrt(jnp.mean(o * o, axis=-1, keepdims=True) + EPS) * sw_ref[...]
        o_ref[0] = (y * (1.0 - lam_init) * _silu(z_ref[0].astype(F32))).astype(o_ref.dtype)

    @pl.when(qi < N_LAT // ATT_TQ)
    def _():
        attend(k_ref[0], v_ref[0])

    @pl.when(qi == N_LAT // ATT_TQ)
    def _():
        attend(k_ref[0, N_LAT:N_TOK, :], v_ref[0, N_LAT:N_TOK, :])


def _diff_attention(lam, qh, kh, proj, subln_w, lam_init):
    bsz = qh.shape[0]
    heads_per_block = D_MODEL // B_V
    tile = lambda c0: pl.BlockSpec((1, ATT_TQ, B_V), lambda b, h, i: (b, i, c0 + h))
    whole = lambda c0: pl.BlockSpec((1, N_TOK, B_V), lambda b, h, i: (b, 0, c0 + h))
    return pl.pallas_call(
        functools.partial(_diff_attn_kernel, lam_init=lam_init),
        grid=(bsz, B_HEADS, N_TOK // ATT_TQ),
        in_specs=[pl.BlockSpec(memory_space=pltpu.SMEM),
                  tile(0), whole(0), whole(COL_BV * heads_per_block),
                  tile(COL_BZ * heads_per_block),
                  pl.BlockSpec((1, B_V), lambda b, h, i: (0, 0))],
        out_specs=tile(0),
        out_shape=jax.ShapeDtypeStruct((bsz, N_TOK, D_MODEL), BF16),
        compiler_params=_cparams("parallel", "parallel", "arbitrary"),
        name="diff_attention",
    )(lam, qh, kh, proj, proj, subln_w)


def _mlstm_kernel(q_ref, k_ref, v_ref, gc_ref, gr_ref, o_ref, c_ref, n_ref, m_ref, *, reverse):
    chunk = MLSTM_CHUNK

    @pl.when(pl.program_id(1) == 0)
    def _():
        c_ref[...] = jnp.zeros_like(c_ref)
        n_ref[...] = jnp.zeros_like(n_ref)
        m_ref[...] = jnp.zeros_like(m_ref)

    row = lax.broadcasted_iota(jnp.int32, (chunk, chunk), 0)
    col = lax.broadcasted_iota(jnp.int32, (chunk, chunk), 1)
    visible = (row <= col) if reverse else (row >= col)
    vis_ts = jnp.where(visible, 1.0, 0.0).astype(BF16)
    vis_st = jnp.where((row >= col) if reverse else (row <= col), 1.0, 0.0).astype(BF16)

    gate_c = gc_ref[0]
    gate_r = gr_ref[0]
    cum_c = _dot_f32_rhs(vis_ts, _log_sigmoid(gate_c))
    logf_r = _log_sigmoid(gate_r)
    cum_r = _dot_f32_lhs(logf_r, vis_st)
    d = 1 if reverse else 0
    neg_inf = jnp.full((chunk, chunk), -jnp.inf, F32)

    for h in range(A_HEADS):
        i_idx = (2 * d) * A_HEADS + h
        f_idx = (2 * d + 1) * A_HEADS + h
        ic_c = gate_c[:, i_idx:i_idx + 1]
        b_c = cum_c[:, f_idx:f_idx + 1]
        ic_r = gate_r[i_idx:i_idx + 1, :]
        b_r = cum_r[f_idx:f_idx + 1, :]
        b_last = jnp.sum(logf_r[f_idx:f_idx + 1, :], axis=1, keepdims=True)
        m_st = m_ref[h, 0:1, 0:1]

        qh = q_ref[0, :, h * A_QK:(h + 1) * A_QK]
        ks = k_ref[0, :, h * A_QK:(h + 1) * A_QK].astype(F32) * (A_QK ** -0.5)
        vh = v_ref[0, :, h * A_V:(h + 1) * A_V]
        c_st = c_ref[h]
        n_st = n_ref[h, 0:1, :]

        dmat = jnp.where(visible, b_c - b_r + ic_r, neg_inf)
        inter = b_c + m_st
        m_row = jnp.maximum(jnp.max(dmat, axis=1, keepdims=True), inter)
        s = lax.dot_general(qh, ks.astype(BF16), (((1,), (1,)), ((), ())),
                            preferred_element_type=F32)
        a = jnp.exp(dmat - m_row) * s
        w_inter = jnp.exp(inter - m_row)
        num = _dot(a.astype(BF16), vh) + w_inter * _dot(qh, c_st.astype(BF16))
        den = (jnp.sum(a, axis=1, keepdims=True)
               + w_inter * jnp.sum(qh.astype(F32) * n_st, axis=1, keepdims=True))
        o_ref[0, :, h * A_V:(h + 1) * A_V] = num / jnp.maximum(jnp.abs(den), jnp.exp(-m_row))

        g = b_last - b_c + ic_c
        m_new = jnp.maximum(b_last + m_st, jnp.max(g, axis=0, keepdims=True))
        kw = ks * jnp.exp(g - m_new)
        decay = jnp.exp(b_last + m_st - m_new)
        c_ref[h] = decay * c_st + _dot(kw.T.astype(BF16), vh)
        n_ref[h] = jnp.broadcast_to(decay * n_st + jnp.sum(kw, axis=0, keepdims=True),
                                    (SUBLANES, A_QK))
        m_ref[h] = jnp.broadcast_to(m_new, (SUBLANES, LANES))


def _mlstm_direction(proj, gates, gates_t, reverse):
    bsz = proj.shape[0]
    n_lat_chunks = N_LAT // MLSTM_CHUNK

    def chunk_of(step):
        latent = (n_lat_chunks - step) if reverse else (step - 1)
        return jnp.where(step == 0, n_lat_chunks, latent)

    return pl.pallas_call(
        functools.partial(_mlstm_kernel, reverse=reverse),
        grid=(bsz, N_TOK // MLSTM_CHUNK),
        in_specs=[
            pl.BlockSpec((1, MLSTM_CHUNK, A_HEADS * A_QK), lambda b, s: (b, chunk_of(s), 0)),
            pl.BlockSpec((1, MLSTM_CHUNK, A_HEADS * A_QK), lambda b, s: (b, chunk_of(s), 1)),
            pl.BlockSpec((1, MLSTM_CHUNK, D_MODEL), lambda b, s: (b, chunk_of(s), COL_AV)),
            pl.BlockSpec((1, MLSTM_CHUNK, LANES), lambda b, s: (b, chunk_of(s), 0)),
            pl.BlockSpec((1, N_GATE, MLSTM_CHUNK), lambda b, s: (b, 0, chunk_of(s))),
        ],
        out_specs=pl.BlockSpec((1, MLSTM_CHUNK, D_MODEL), lambda b, s: (b, chunk_of(s), 0)),
        out_shape=jax.ShapeDtypeStruct((bsz, N_TOK, D_MODEL), F32),
        scratch_shapes=[pltpu.VMEM((A_HEADS, A_QK, A_V), F32),
                        pltpu.VMEM((A_HEADS, SUBLANES, A_QK), F32),
                        pltpu.VMEM((A_HEADS, SUBLANES, LANES), F32)],
        compiler_params=_cparams("parallel", "arbitrary"),
        name="mlstm_bwd" if reverse else "mlstm_fwd",
    )(proj, proj, proj, gates, gates_t)


def _mlstm_out_kernel(hf_ref, hb_ref, o_ref, z_ref, w_ref, y_ref):
    for h in range(A_HEADS):
        cols = slice(h * A_V, (h + 1) * A_V)
        x = hf_ref[0, :, cols] + hb_ref[0, :, cols]
        xn = x * lax.rsqrt(jnp.mean(x * x, axis=-1, keepdims=True) + EPS) * w_ref[:, cols]
        gate = _sigmoid(o_ref[0, :, cols].astype(F32)) * _silu(z_ref[0, :, cols].astype(F32))
        y_ref[0, :, cols] = (xn * gate).astype(y_ref.dtype)


def _mlstm_out(h_fwd, h_bwd, proj, a_norm_w):
    bsz = proj.shape[0]
    col = lambda c: pl.BlockSpec((1, TOK_TILE, D_MODEL), lambda b, t: (b, t, c))
    return pl.pallas_call(
        _mlstm_out_kernel,
        grid=(bsz, N_TOK // TOK_TILE),
        in_specs=[col(0), col(0), col(COL_AO), col(COL_AZ),
                  pl.BlockSpec((1, D_MODEL), lambda b, t: (0, 0))],
        out_specs=col(0),
        out_shape=jax.ShapeDtypeStruct((bsz, N_TOK, D_MODEL), BF16),
        compiler_params=_cparams("parallel", "parallel"),
        name="mlstm_out",
    )(h_fwd, h_bwd, proj, proj, a_norm_w)


def _chan_dft_kernel(u_ref, cs_ref, pq_ref):
    for g in range(C_GROUPS):
        cols = slice(g * C_GROUP_DIM, (g + 1) * C_GROUP_DIM)
        r = _dot(u_ref[0, :, cols], cs_ref[...])
        pq_ref[0, 0, :, cols] = r[:, :C_GROUP_DIM].astype(pq_ref.dtype)
        pq_ref[0, 1, :, cols] = r[:, C_GROUP_DIM:].astype(pq_ref.dtype)


def _chan_dft(proj, cs):
    bsz = proj.shape[0]
    return pl.pallas_call(
        _chan_dft_kernel,
        grid=(bsz, N_TOK // TOK_TILE),
        in_specs=[pl.BlockSpec((1, TOK_TILE, D_MODEL), lambda b, t: (b, t, COL_CU)),
                  pl.BlockSpec((C_GROUP_DIM, 2 * C_GROUP_DIM), lambda b, t: (0, 0))],
        out_specs=pl.BlockSpec((1, 2, TOK_TILE, D_MODEL), lambda b, t: (b, 0, t, 0)),
        out_shape=jax.ShapeDtypeStruct((bsz, 2, N_TOK, D_MODEL), BF16),
        compiler_params=_cparams("parallel", "parallel"),
        name="fourier_channel_dft",
    )(proj, cs)


def _seq_dft_kernel(a_ref, pq_ref, z_ref, y_ref, acc_ref):
    k = pl.program_id(2)

    @pl.when(k == 0)
    def _():
        acc_ref[...] = jnp.zeros_like(acc_ref)

    acc_ref[...] += _dot(a_ref[...], pq_ref[0, 0])

    @pl.when(k == pl.num_programs(2) - 1)
    def _():
        y_ref[0] = (acc_ref[...] * _silu(z_ref[0].astype(F32))).astype(y_ref.dtype)


def _seq_dft_latent(dft_lat, pq, proj):
    bsz = proj.shape[0]
    k_per_part = N_LAT // DFT_TK
    return pl.pallas_call(
        _seq_dft_kernel,
        grid=(bsz, N_LAT // DFT_TM, 2 * k_per_part),
        in_specs=[
            pl.BlockSpec((DFT_TM, DFT_TK), lambda b, i, k: (i, k)),
            pl.BlockSpec((1, 1, DFT_TK, D_MODEL),
                         lambda b, i, k: (b, k // k_per_part, k % k_per_part, 0)),
            pl.BlockSpec((1, DFT_TM, D_MODEL), lambda b, i, k: (b, i, COL_CZ)),
        ],
        out_specs=pl.BlockSpec((1, DFT_TM, D_MODEL), lambda b, i, k: (b, i, 0)),
        out_shape=jax.ShapeDtypeStruct((bsz, N_TOK, D_MODEL), BF16),
        scratch_shapes=[pltpu.VMEM((DFT_TM, D_MODEL), F32)],
        compiler_params=_cparams("parallel", "parallel", "arbitrary"),
        name="fourier_token_dft_latent",
    )(dft_lat, pq, proj)


def _seq_dft_ctx_kernel(a_ref, pq_ref, z_ref, y_in_ref, y_ref):
    del y_in_ref
    r = _dot(a_ref[:, :N_CTX], pq_ref[0, 0]) + _dot(a_ref[:, N_CTX:], pq_ref[0, 1])
    y_ref[0] = (r * _silu(z_ref[0].astype(F32))).astype(y_ref.dtype)


def _seq_dft_ctx(dft_ctx, pq, proj, y_c):
    bsz = proj.shape[0]
    ctx_blk = N_LAT // N_CTX
    return pl.pallas_call(
        _seq_dft_ctx_kernel,
        grid=(bsz,),
        in_specs=[
            pl.BlockSpec((N_CTX, 2 * N_CTX), lambda b: (0, 0)),
            pl.BlockSpec((1, 2, N_CTX, D_MODEL), lambda b: (b, 0, ctx_blk, 0)),
            pl.BlockSpec((1, N_CTX, D_MODEL), lambda b: (b, ctx_blk, COL_CZ)),
            pl.BlockSpec(memory_space=pl.ANY),
        ],
        out_specs=pl.BlockSpec((1, N_CTX, D_MODEL), lambda b: (b, ctx_blk, 0)),
        out_shape=jax.ShapeDtypeStruct(y_c.shape, y_c.dtype),
        input_output_aliases={3: 0},
        compiler_params=_cparams("parallel"),
        name="fourier_token_dft_ctx",
    )(dft_ctx, pq, proj, y_c)


def _merge_kernel(x_ref, ya_ref, yb_ref, yc_ref, ga_ref, gb_ref, gc_ref, mod_ref,
                  wa_ref, wb_ref, wc_ref, wo_ref, o_ref):
    b = pl.program_id(0)

    def branch(y_ref, g_ref, w_ref):
        return _sigmoid(g_ref[0].astype(F32)) * _dot(y_ref[0], w_ref[...])

    y = branch(ya_ref, ga_ref, wa_ref) + branch(yb_ref, gb_ref, wb_ref) + branch(yc_ref, gc_ref, wc_ref)
    out = _dot(y.astype(BF16), wo_ref[...])
    gate_cols = slice(2 * D_MODEL, 3 * D_MODEL)
    gate = jnp.where(_ctx_rows(pl.program_id(1), TOK_TILE),
                     mod_ref[pl.ds(B_CTX_ROW, 1), gate_cols], mod_ref[pl.ds(b, 1), gate_cols])
    o_ref[0] = x_ref[0] + gate * out


def _merge(xs, y_a, y_b, y_c, proj, mod_l, wa, wb, wc, wo):
    bsz = xs.shape[0]
    col = lambda c: pl.BlockSpec((1, TOK_TILE, D_MODEL), lambda b, t: (b, t, c))
    weight = pl.BlockSpec((D_MODEL, D_MODEL), lambda b, t: (0, 0))
    return pl.pallas_call(
        _merge_kernel,
        grid=(bsz, N_TOK // TOK_TILE),
        in_specs=[col(0), col(0), col(0), col(0), col(COL_MG), col(COL_MG + 1), col(COL_MG + 2),
                  pl.BlockSpec((SUBLANES, 3 * D_MODEL), lambda b, t: (0, 0)),
                  weight, weight, weight, weight],
        out_specs=col(0),
        out_shape=jax.ShapeDtypeStruct(xs.shape, F32),
        compiler_params=_cparams("parallel", "parallel"),
        name="merge_out_residual",
    )(xs, y_a, y_b, y_c, proj, proj, proj, mod_l, wa, wb, wc, wo)


def _rope_tables():
    n = jnp.arange(N_LAT, dtype=jnp.int32)
    row = (n // GRID_W).astype(F32)
    col = (n % GRID_W).astype(F32)
    inv_freq = ROPE_THETA ** (-jnp.arange(0, ROPE_AXIS_DIM, 2, dtype=F32) / ROPE_AXIS_DIM)
    ang_r = row[:, None] * inv_freq
    ang_c = col[:, None] * inv_freq
    cos = jnp.concatenate([jnp.cos(ang_r)] * 2 + [jnp.cos(ang_c)] * 2, axis=-1)
    sin = jnp.concatenate([-jnp.sin(ang_r), jnp.sin(ang_r), -jnp.sin(ang_c), jnp.sin(ang_c)], axis=-1)
    cos = jnp.concatenate([cos, jnp.ones((N_CTX, B_QK), F32)], axis=0)
    sin = jnp.concatenate([sin, jnp.zeros((N_CTX, B_QK), F32)], axis=0)
    reps = LANES // B_QK
    return jnp.tile(cos, (1, reps)), jnp.tile(sin, (1, reps))


def _dft_cos_sin(n, scale):
    idx = jnp.arange(n, dtype=jnp.int32)
    ang = ((idx[:, None] * idx[None, :]) % n).astype(F32) * (2.0 * math.pi / n)
    return jnp.cos(ang) * scale, jnp.sin(ang) * scale


def _dft_tables():
    c_ch, s_ch = _dft_cos_sin(C_GROUP_DIM, 1.0)
    chan = jnp.concatenate([c_ch, s_ch], axis=1).astype(BF16)
    c_l, s_l = _dft_cos_sin(N_LAT, (N_LAT * C_GROUP_DIM) ** -0.5)
    lat = jnp.concatenate([c_l, -s_l], axis=1).astype(BF16)
    c_c, s_c = _dft_cos_sin(N_CTX, (N_CTX * C_GROUP_DIM) ** -0.5)
    ctx = jnp.concatenate([c_c, -s_c], axis=1).astype(BF16)
    return chan, lat, ctx


def _group_sum_matrix():
    g = np.arange(LANES) // B_QK
    return jnp.asarray(g[:, None] == g[None, :], dtype=BF16)


def kernel(x, c, ctx, c_ctx, norm_w, w_ada, b_ada, w_in, b_if, a_norm_w, q_norm_w, k_norm_w,
           lambda_q1, lambda_k1, lambda_q2, lambda_k2, subln_w, w_a_out, w_b_out, w_c_out, w_out):
    bsz = x.shape[0]
    xs = jnp.concatenate([x, ctx], axis=1)
    c_rows = jnp.concatenate(
        [c, c_ctx[None, :], jnp.zeros((SUBLANES - bsz - 1, D_MODEL), F32)], axis=0)
    mod, lam = _modulation(c_rows, w_ada, b_ada, lambda_q1, lambda_k1, lambda_q2, lambda_k2)

    cos_t, sin_t = _rope_tables()
    dft_chan, dft_lat, dft_ctx = _dft_tables()
    group_sum = _group_sum_matrix()
    lane_pad = LANES - N_GATE

    for l in range(DEPTH):
        w_proj = jnp.concatenate(
            [w_in[l, :, :GATE_COL0], w_in[l, :, GATE_COL0 + N_GATE:]], axis=1).astype(BF16)
        w_gate = jnp.pad(w_in[l, :, GATE_COL0:GATE_COL0 + N_GATE], ((0, 0), (0, lane_pad)))
        b_gate = jnp.pad(b_if[l][None, :], ((0, 0), (0, lane_pad)))

        h, gates = _norm_mod(xs, norm_w[l][None, :], mod[l], w_gate, b_gate)
        proj = _in_projection(h.reshape(bsz * N_TOK, D_MODEL), w_proj).reshape(bsz, N_TOK, N_PROJ)

        gates_t = jnp.swapaxes(gates[:, :, :N_GATE], 1, 2)
        h_fwd = _mlstm_direction(proj, gates, gates_t, reverse=False)
        h_bwd = _mlstm_direction(proj, gates, gates_t, reverse=True)
        y_a = _mlstm_out(h_fwd, h_bwd, proj, a_norm_w[l][None, :])

        tile2 = lambda w: jnp.tile(w[None, :], (1, LANES // B_QK))
        qh, kh = _qk_prep(proj, cos_t, sin_t, tile2(q_norm_w[l]), tile2(k_norm_w[l]), group_sum)
        lam_init = 0.8 - 0.6 * math.exp(-0.3 * l)
        y_b = _diff_attention(lam[l, 0, :1], qh, kh, proj, subln_w[l][None, :], lam_init)

        pq = _chan_dft(proj, dft_chan)
        y_c = _seq_dft_ctx(dft_ctx, pq, proj, _seq_dft_latent(dft_lat, pq, proj))

        xs = _merge(xs, y_a, y_b, y_c, proj, mod[l],
                    w_a_out[l].astype(BF16), w_b_out[l].astype(BF16),
                    w_c_out[l].astype(BF16), w_out[l].astype(BF16))
    return xs[:, :N_LAT, :]
```

```python
import functools
import math

import jax
import jax.numpy as jnp
import numpy as np
from jax import lax
from jax.experimental import pallas as pl
from jax.experimental.pallas import tpu as pltpu

F32 = jnp.float32
BF16 = jnp.bfloat16
LOG2E = math.log2(math.e)

D_MODEL = 1024
DEPTH = 4
N_LAT = 4096
N_CTX = 256
N_TOK = N_LAT + N_CTX
GRID_W = 64
EPS = 1e-6

A_HEADS, A_QK, A_V = 4, 128, 256
B_HEADS, B_QK, B_V = 8, 64, 128
C_GROUPS, C_GROUP_DIM = 4, 256
ROPE_AXIS_DIM = B_QK // 2
ROPE_THETA = 10000.0
N_GATE = 4 * A_HEADS
BATCH = 4
B_CTX_ROW = BATCH

COL_AQK, COL_AV, COL_AO, COL_AZ, COL_BQ, COL_BK, COL_BV, COL_BZ, COL_CU, COL_CZ, COL_MG = (
    0, 1, 2, 3, 4, 5, 6, 7, 8, 9, 10)
N_PROJ = 13 * D_MODEL
GATE_COL0 = 2 * A_HEADS * A_QK + A_HEADS * A_V

LANES = 128
SUBLANES = 8
VMEM_LIMIT_BYTES = 56 * 1024 * 1024

TOK_TILE = 544
MM_TM = 1024
MM_TN = 1024
ATT_TQ = 256
ATT_ROWS = 2 * ATT_TQ
QK_TILE = 256
LAST_TILE = 512
MLSTM_CHUNK = 256
DFT_TM = 1024
DFT_TK = 2048


def _cparams(*sem):
    return pltpu.CompilerParams(dimension_semantics=sem, vmem_limit_bytes=VMEM_LIMIT_BYTES)


def _split3(x):
    hi = x.astype(BF16)
    r1 = x - hi.astype(F32)
    mid = r1.astype(BF16)
    lo = (r1 - mid.astype(F32)).astype(BF16)
    return hi, mid, lo


def _dot(a, b):
    return jnp.dot(a, b, preferred_element_type=F32)


def _dot_f32_rhs(a_exact, b):
    hi, mid, lo = _split3(b)
    return _dot(a_exact, hi) + _dot(a_exact, mid) + _dot(a_exact, lo)


def _dot_f32_lhs(a, b_exact):
    hi, mid, lo = _split3(a)
    return _dot(hi, b_exact) + _dot(mid, b_exact) + _dot(lo, b_exact)


def _dot_f32(a, b):
    a_hi = a.astype(BF16)
    a_lo = (a - a_hi.astype(F32)).astype(BF16)
    b_hi = b.astype(BF16)
    b_lo = (b - b_hi.astype(F32)).astype(BF16)
    return _dot(a_hi, b_hi) + _dot(a_lo, b_hi) + _dot(a_hi, b_lo)


def _sigmoid(x):
    return 1.0 / (1.0 + jnp.exp(-x))


def _silu(x):
    return x * _sigmoid(x)


def _log_sigmoid(x):
    return jnp.minimum(x, 0.0) - jnp.log(1.0 + jnp.exp(-jnp.abs(x)))


def _mod_kernel(c_ref, w_ref, b_ref, q1_ref, k1_ref, q2_ref, k2_ref, mod_ref, lam_ref):
    sc = _silu(c_ref[...])
    mod_ref[0] = _dot_f32(sc, w_ref[0]) + b_ref[0]
    layer = jnp.full((1, LANES), pl.program_id(0), dtype=jnp.int32).astype(F32)
    lam_init = 0.8 - 0.6 * jnp.exp(-0.3 * layer)
    s1 = jnp.sum(q1_ref[0] * k1_ref[0], axis=-1, keepdims=True)
    s2 = jnp.sum(q2_ref[0] * k2_ref[0], axis=-1, keepdims=True)
    lam_ref[0] = jnp.exp(s1) - jnp.exp(s2) + lam_init


def _modulation(c_rows, w_ada, b_ada, lq1, lk1, lq2, lk2):
    n_col = 3
    vec = lambda a: a.reshape(DEPTH, 1, B_QK)
    lam_spec = pl.BlockSpec((1, 1, B_QK), lambda l, j: (l, 0, 0))
    return pl.pallas_call(
        _mod_kernel,
        grid=(DEPTH, n_col),
        in_specs=[
            pl.BlockSpec((SUBLANES, D_MODEL), lambda l, j: (0, 0)),
            pl.BlockSpec((1, D_MODEL, D_MODEL), lambda l, j: (l, 0, j)),
            pl.BlockSpec((1, 1, D_MODEL), lambda l, j: (l, 0, j)),
            lam_spec, lam_spec, lam_spec, lam_spec,
        ],
        out_specs=[
            pl.BlockSpec((1, SUBLANES, D_MODEL), lambda l, j: (l, 0, j)),
            pl.BlockSpec((1, 1, LANES), lambda l, j: (l, 0, 0)),
        ],
        out_shape=[
            jax.ShapeDtypeStruct((DEPTH, SUBLANES, 3 * D_MODEL), F32),
            jax.ShapeDtypeStruct((DEPTH, 1, LANES), F32),
        ],
        compiler_params=_cparams("arbitrary", "arbitrary"),
        name="adaln_modulation",
    )(c_rows, w_ada, b_ada.reshape(DEPTH, 1, 3 * D_MODEL), vec(lq1), vec(lk1), vec(lq2), vec(lk2))


def _ctx_rows(tile_idx, tile):
    row = tile_idx * tile + lax.broadcasted_iota(jnp.int32, (tile, 1), 0)
    return row >= N_LAT


def _norm_mod_kernel(x_ref, nw_ref, mod_ref, wg_ref, bif_ref, h_ref, g_ref):
    b = pl.program_id(0)
    x = x_ref[0]
    y = x * lax.rsqrt(jnp.mean(x * x, axis=-1, keepdims=True) + EPS) * nw_ref[...]
    is_ctx = _ctx_rows(pl.program_id(1), TOK_TILE)
    m_lat = mod_ref[pl.ds(b, 1), :]
    m_ctx = mod_ref[pl.ds(B_CTX_ROW, 1), :]
    shift = jnp.where(is_ctx, m_ctx[:, 0:D_MODEL], m_lat[:, 0:D_MODEL])
    scale = jnp.where(is_ctx, m_ctx[:, D_MODEL:2 * D_MODEL], m_lat[:, D_MODEL:2 * D_MODEL])
    h = y * (1.0 + scale) + shift
    h_ref[0] = h.astype(BF16)
    g_ref[0] = _dot_f32(h, wg_ref[...]) + bif_ref[...]


def _norm_mod(xs, norm_w, mod_l, w_gate, b_if):
    bsz = xs.shape[0]
    tok = pl.BlockSpec((1, TOK_TILE, D_MODEL), lambda b, t: (b, t, 0))
    full = lambda shape: pl.BlockSpec(shape, lambda b, t: (0,) * len(shape))
    return pl.pallas_call(
        _norm_mod_kernel,
        grid=(bsz, N_TOK // TOK_TILE),
        in_specs=[tok, full((1, D_MODEL)), full((SUBLANES, 3 * D_MODEL)),
                  full((D_MODEL, LANES)), full((1, LANES))],
        out_specs=[tok, pl.BlockSpec((1, TOK_TILE, LANES), lambda b, t: (b, t, 0))],
        out_shape=[jax.ShapeDtypeStruct((bsz, N_TOK, D_MODEL), BF16),
                   jax.ShapeDtypeStruct((bsz, N_TOK, LANES), F32)],
        compiler_params=_cparams("parallel", "parallel"),
        name="norm_modulate",
    )(xs, norm_w, mod_l, w_gate, b_if)


def _matmul_kernel(a_ref, w_ref, o_ref):
    o_ref[...] = _dot(a_ref[...], w_ref[...]).astype(o_ref.dtype)


def _in_projection(h2d, w):
    m, k = h2d.shape
    n = w.shape[1]
    return pl.pallas_call(
        _matmul_kernel,
        grid=(m // MM_TM, n // MM_TN),
        in_specs=[pl.BlockSpec((MM_TM, k), lambda i, j: (i, 0)),
                  pl.BlockSpec((k, MM_TN), lambda i, j: (0, j))],
        out_specs=pl.BlockSpec((MM_TM, MM_TN), lambda i, j: (i, j)),
        out_shape=jax.ShapeDtypeStruct((m, n), BF16),
        compiler_params=_cparams("parallel", "arbitrary"),
        name="in_projection",
    )(h2d, w)


def _qk_prep_kernel(q_ref, k_ref, cos_ref, sin_ref, qw_ref, kw_ref, g_ref, qo_ref, kt_ref):
    cos = cos_ref[...]
    sin = sin_ref[...]
    group_sum = g_ref[...]
    lane = lax.broadcasted_iota(jnp.int32, (1, LANES), 1)
    first_half = (lane % ROPE_AXIS_DIM) < (ROPE_AXIS_DIM // 2)

    def prep(x_bf16, w):
        x = x_bf16.astype(F32)
        ms = _dot_f32_lhs(x * x, group_sum) * (1.0 / B_QK)
        y = x * lax.rsqrt(ms + EPS) * w
        half = ROPE_AXIS_DIM // 2
        partner = jnp.where(first_half, pltpu.roll(y, LANES - half, 1), pltpu.roll(y, half, 1))
        return y * cos + partner * sin

    for s in range(D_MODEL // LANES):
        cols = slice(s * LANES, (s + 1) * LANES)
        qo_ref[0, :, cols] = (prep(q_ref[0, :, cols], qw_ref[...]) * (B_QK ** -0.5 * LOG2E)).astype(BF16)
        kt_ref[0, cols, :] = prep(k_ref[0, :, cols], kw_ref[...]).T.astype(BF16)


def _qk_prep(proj, cos_t, sin_t, qw, kw, group_sum):
    bsz = proj.shape[0]
    col = lambda c: pl.BlockSpec((1, QK_TILE, D_MODEL), lambda b, t: (b, t, c))
    tab = pl.BlockSpec((QK_TILE, LANES), lambda b, t: (t, 0))
    full = lambda shape: pl.BlockSpec(shape, lambda b, t: (0,) * len(shape))
    return pl.pallas_call(
        _qk_prep_kernel,
        grid=(bsz, N_TOK // QK_TILE),
        in_specs=[col(COL_BQ), col(COL_BK), tab, tab, full((1, LANES)), full((1, LANES)),
                  full((LANES, LANES))],
        out_specs=[pl.BlockSpec((1, QK_TILE, D_MODEL), lambda b, t: (b, t, 0)),
                   pl.BlockSpec((1, D_MODEL, QK_TILE), lambda b, t: (b, 0, t))],
        out_shape=[jax.ShapeDtypeStruct((bsz, N_TOK, D_MODEL), BF16),
                   jax.ShapeDtypeStruct((bsz, D_MODEL, N_TOK), BF16)],
        compiler_params=_cparams("parallel", "parallel"),
        name="qk_norm_rope",
    )(proj, proj, cos_t, sin_t, qw, kw, group_sum)


def _diff_attn_body(lam_ref, q_ref, kt_ref, v_ref, z_ref, sw_ref, o_ref, vp_ref, lam_init):
    @pl.when(pl.program_id(2) == 0)
    def _():
        vp_ref[:, :B_V] = v_ref[0]
        vp_ref[:, B_V:] = jnp.ones((vp_ref.shape[0], B_V), BF16)

    chains = [slice(c * ATT_TQ, (c + 1) * ATT_TQ) for c in range(q_ref.shape[1] // ATT_TQ)]
    lane = lax.broadcasted_iota(jnp.int32, (ATT_TQ, B_V), 1)
    zero = jnp.zeros((ATT_TQ, B_V), BF16)

    def scores(rows, first_map):
        q = q_ref[0, rows, :]
        return _dot(jnp.where((lane < B_QK) if first_map else (lane >= B_QK), q, zero), kt_ref[0])

    def numerators(s):
        return jnp.exp2(s - jnp.max(s, axis=-1, keepdims=True)).astype(BF16)

    s0 = [scores(rows, True) for rows in chains]
    s1 = [scores(rows, False) for rows in chains]
    u0 = [_dot(numerators(s), vp_ref[...]) for s in s0]
    u1 = [_dot(numerators(s), vp_ref[...]) for s in s1]
    for rows, a, b in zip(chains, u0, u1):
        o = a[:, :B_V] / a[:, B_V:] - lam_ref[0] * (b[:, :B_V] / b[:, B_V:])
        y = o * lax.rsqrt(jnp.mean(o * o, axis=-1, keepdims=True) + EPS) * sw_ref[...]
        o_ref[0, rows, :] = (y * (1.0 - lam_init)
                             * _silu(z_ref[0, rows, :].astype(F32))).astype(o_ref.dtype)


def _diff_attn_kernel(lam_ref, q_ref, kt_ref, v_ref, z_ref, sw_ref, o_ref, vp_ref, *, lam_init):
    _diff_attn_body(lam_ref, q_ref, kt_ref, v_ref, z_ref, sw_ref, o_ref, vp_ref, lam_init)


def _diff_attn_ctx_kernel(lam_ref, q_ref, kt_ref, v_ref, z_ref, sw_ref, y_in_ref, o_ref, vp_ref,
                          *, lam_init):
    del y_in_ref
    _diff_attn_body(lam_ref, q_ref, kt_ref, v_ref, z_ref, sw_ref, o_ref, vp_ref, lam_init)


def _diff_attention(lam, qh, kt, proj, subln_w, lam_init, y_latent=None):
    bsz = qh.shape[0]
    heads_per_block = D_MODEL // B_V
    ctx = y_latent is not None
    n_keys = N_CTX if ctx else N_TOK
    rows = N_CTX if ctx else ATT_ROWS
    q0 = N_LAT // rows if ctx else 0
    k0 = N_LAT // N_CTX if ctx else 0
    tile = lambda c0: pl.BlockSpec((1, rows, B_V), lambda b, h, i: (b, q0 + i, c0 + h))
    in_specs = [pl.BlockSpec(memory_space=pltpu.SMEM),
                tile(0),
                pl.BlockSpec((1, B_V, n_keys), lambda b, h, i: (b, h, k0)),
                pl.BlockSpec((1, n_keys, B_V), lambda b, h, i: (b, k0, COL_BV * heads_per_block + h)),
                tile(COL_BZ * heads_per_block),
                pl.BlockSpec((1, B_V), lambda b, h, i: (0, 0))]
    args = [lam, qh, kt, proj, proj, subln_w]
    if ctx:
        in_specs.append(pl.BlockSpec(memory_space=pl.ANY))
        args.append(y_latent)
    return pl.pallas_call(
        functools.partial(_diff_attn_ctx_kernel if ctx else _diff_attn_kernel, lam_init=lam_init),
        grid=(bsz, B_HEADS, (N_CTX if ctx else N_LAT) // rows),
        in_specs=in_specs,
        out_specs=tile(0),
        out_shape=jax.ShapeDtypeStruct((bsz, N_TOK, D_MODEL), BF16),
        scratch_shapes=[pltpu.VMEM((n_keys, 2 * B_V), BF16)],
        input_output_aliases={6: 0} if ctx else {},
        compiler_params=_cparams("parallel", "parallel", "arbitrary"),
        name="diff_attention_ctx" if ctx else "diff_attention",
    )(*args)


def _mlstm_gates(gc_ref, gr_ref, reverse):
    chunk = MLSTM_CHUNK
    row = lax.broadcasted_iota(jnp.int32, (chunk, chunk), 0)
    col = lax.broadcasted_iota(jnp.int32, (chunk, chunk), 1)
    visible = (row <= col) if reverse else (row >= col)
    vis_ts = jnp.where(visible, 1.0, 0.0).astype(BF16)
    vis_st = jnp.where((row >= col) if reverse else (row <= col), 1.0, 0.0).astype(BF16)
    gate_c = gc_ref[0]
    gate_r = gr_ref[0]
    logf_r = _log_sigmoid(gate_r)
    return dict(visible=visible, gate_c=gate_c, gate_r=gate_r, logf_r=logf_r,
                cum_c=_dot_f32_rhs(vis_ts, _log_sigmoid(gate_c)),
                cum_r=_dot_f32_lhs(logf_r, vis_st))


def _mlstm_scores(g, h, reverse, q_ref, k_ref, m_ref):
    d = 1 if reverse else 0
    i_idx = (2 * d) * A_HEADS + h
    f_idx = (2 * d + 1) * A_HEADS + h
    b_c = g["cum_c"][:, f_idx:f_idx + 1]
    ic_r = g["gate_r"][i_idx:i_idx + 1, :]
    b_r = g["cum_r"][f_idx:f_idx + 1, :]
    m_st = m_ref[h, 0:1, 0:1]
    qh = q_ref[0, :, h * A_QK:(h + 1) * A_QK]
    ks = k_ref[0, :, h * A_QK:(h + 1) * A_QK].astype(F32) * (A_QK ** -0.5)
    dmat = jnp.where(g["visible"], b_c - b_r + ic_r, -jnp.inf)
    inter = b_c + m_st
    m_row = jnp.maximum(jnp.max(dmat, axis=1, keepdims=True), inter)
    s = lax.dot_general(qh, ks.astype(BF16), (((1,), (1,)), ((), ())), preferred_element_type=F32)
    return dict(
        qh=qh, ks=ks, m_st=m_st, b_c=b_c, m_row=m_row,
        ic_c=g["gate_c"][:, i_idx:i_idx + 1],
        b_last=jnp.sum(g["logf_r"][f_idx:f_idx + 1, :], axis=1, keepdims=True),
        a=jnp.exp(dmat - m_row) * s, w_inter=jnp.exp(inter - m_row))


def _mlstm_output(t, h, v_ref, o_ref, c_ref, n_ref):
    vh = v_ref[0, :, h * A_V:(h + 1) * A_V]
    n_st = n_ref[h, 0:1, :]
    num = _dot(t["a"].astype(BF16), vh) + t["w_inter"] * _dot(t["qh"], c_ref[h].astype(BF16))
    den = (jnp.sum(t["a"], axis=1, keepdims=True)
           + t["w_inter"] * jnp.sum(t["qh"].astype(F32) * n_st, axis=1, keepdims=True))
    o_ref[0, :, h * A_V:(h + 1) * A_V] = num / jnp.maximum(jnp.abs(den), jnp.exp(-t["m_row"]))


def _mlstm_state(t, h, v_ref, c_ref, n_ref, m_ref):
    vh = v_ref[0, :, h * A_V:(h + 1) * A_V]
    g = t["b_last"] - t["b_c"] + t["ic_c"]
    m_new = jnp.maximum(t["b_last"] + t["m_st"], jnp.max(g, axis=0, keepdims=True))
    kw = t["ks"] * jnp.exp(g - m_new)
    decay = jnp.exp(t["b_last"] + t["m_st"] - m_new)
    c_ref[h] = decay * c_ref[h] + _dot(kw.T.astype(BF16), vh)
    n_ref[h] = jnp.broadcast_to(decay * n_ref[h, 0:1, :] + jnp.sum(kw, axis=0, keepdims=True),
                                (SUBLANES, A_QK))
    m_ref[h] = jnp.broadcast_to(m_new, (SUBLANES, LANES))


def _mlstm_kernel(*refs):
    n_in, n_state = 5, 3
    ins = (refs[:n_in], refs[n_in:2 * n_in])
    outs = (refs[2 * n_in], refs[2 * n_in + 1])
    state = refs[2 * n_in + 2:]
    states = (state[:n_state], state[n_state:])

    @pl.when(pl.program_id(1) == 0)
    def _():
        for ref in state:
            ref[...] = jnp.zeros_like(ref)

    gates = [_mlstm_gates(ins[d][3], ins[d][4], reverse=bool(d)) for d in range(2)]
    chains = [(d, h) for h in range(A_HEADS) for d in range(2)]
    terms = {}
    for d, h in chains:
        q_ref, k_ref = ins[d][0], ins[d][1]
        terms[d, h] = _mlstm_scores(gates[d], h, bool(d), q_ref, k_ref, states[d][2])
    for d, h in chains:
        c_ref, n_ref, _ = states[d]
        _mlstm_output(terms[d, h], h, ins[d][2], outs[d], c_ref, n_ref)
    for d, h in chains:
        c_ref, n_ref, m_ref = states[d]
        _mlstm_state(terms[d, h], h, ins[d][2], c_ref, n_ref, m_ref)


def _mlstm(proj, gates, gates_t):
    bsz = proj.shape[0]
    n_lat_chunks = N_LAT // MLSTM_CHUNK

    def chunk_of(step, reverse):
        latent = (n_lat_chunks - step) if reverse else (step - 1)
        return jnp.where(step == 0, n_lat_chunks, latent)

    def specs(reverse):
        ck = functools.partial(chunk_of, reverse=reverse)
        return [
            pl.BlockSpec((1, MLSTM_CHUNK, A_HEADS * A_QK), lambda b, s: (b, ck(s), 0)),
            pl.BlockSpec((1, MLSTM_CHUNK, A_HEADS * A_QK), lambda b, s: (b, ck(s), 1)),
            pl.BlockSpec((1, MLSTM_CHUNK, D_MODEL), lambda b, s: (b, ck(s), COL_AV)),
            pl.BlockSpec((1, MLSTM_CHUNK, LANES), lambda b, s: (b, ck(s), 0)),
            pl.BlockSpec((1, N_GATE, MLSTM_CHUNK), lambda b, s: (b, 0, ck(s))),
        ]

    def out_spec(reverse):
        ck = functools.partial(chunk_of, reverse=reverse)
        return pl.BlockSpec((1, MLSTM_CHUNK, D_MODEL), lambda b, s: (b, ck(s), 0))

    state = [pltpu.VMEM((A_HEADS, A_QK, A_V), F32),
             pltpu.VMEM((A_HEADS, SUBLANES, A_QK), F32),
             pltpu.VMEM((A_HEADS, SUBLANES, LANES), F32)]
    args = (proj, proj, proj, gates, gates_t)
    return pl.pallas_call(
        _mlstm_kernel,
        grid=(bsz, N_TOK // MLSTM_CHUNK),
        in_specs=specs(False) + specs(True),
        out_specs=[out_spec(False), out_spec(True)],
        out_shape=[jax.ShapeDtypeStruct((bsz, N_TOK, D_MODEL), F32)] * 2,
        scratch_shapes=state + state,
        compiler_params=_cparams("parallel", "arbitrary"),
        name="mlstm_bidirectional",
    )(*args, *args)


def _mlstm_out_kernel(hf_ref, hb_ref, o_ref, z_ref, w_ref, y_ref):
    for h in range(A_HEADS):
        cols = slice(h * A_V, (h + 1) * A_V)
        x = hf_ref[0, :, cols] + hb_ref[0, :, cols]
        xn = x * lax.rsqrt(jnp.mean(x * x, axis=-1, keepdims=True) + EPS) * w_ref[:, cols]
        gate = _sigmoid(o_ref[0, :, cols].astype(F32)) * _silu(z_ref[0, :, cols].astype(F32))
        y_ref[0, :, cols] = (xn * gate).astype(y_ref.dtype)


def _mlstm_out(h_fwd, h_bwd, proj, a_norm_w):
    bsz = proj.shape[0]
    col = lambda c: pl.BlockSpec((1, TOK_TILE, D_MODEL), lambda b, t: (b, t, c))
    return pl.pallas_call(
        _mlstm_out_kernel,
        grid=(bsz, N_TOK // TOK_TILE),
        in_specs=[col(0), col(0), col(COL_AO), col(COL_AZ),
                  pl.BlockSpec((1, D_MODEL), lambda b, t: (0, 0))],
        out_specs=col(0),
        out_shape=jax.ShapeDtypeStruct((bsz, N_TOK, D_MODEL), BF16),
        compiler_params=_cparams("parallel", "parallel"),
        name="mlstm_out",
    )(h_fwd, h_bwd, proj, proj, a_norm_w)


def _chan_dft_kernel(u_ref, cs_ref, pq_ref):
    for g in range(C_GROUPS):
        cols = slice(g * C_GROUP_DIM, (g + 1) * C_GROUP_DIM)
        r = _dot(u_ref[0, :, cols], cs_ref[...])
        pq_ref[0, 0, :, cols] = r[:, :C_GROUP_DIM].astype(pq_ref.dtype)
        pq_ref[0, 1, :, cols] = r[:, C_GROUP_DIM:].astype(pq_ref.dtype)


def _chan_dft(proj, cs):
    bsz = proj.shape[0]
    return pl.pallas_call(
        _chan_dft_kernel,
        grid=(bsz, N_TOK // TOK_TILE),
        in_specs=[pl.BlockSpec((1, TOK_TILE, D_MODEL), lambda b, t: (b, t, COL_CU)),
                  pl.BlockSpec((C_GROUP_DIM, 2 * C_GROUP_DIM), lambda b, t: (0, 0))],
        out_specs=pl.BlockSpec((1, 2, TOK_TILE, D_MODEL), lambda b, t: (b, 0, t, 0)),
        out_shape=jax.ShapeDtypeStruct((bsz, 2, N_TOK, D_MODEL), BF16),
        compiler_params=_cparams("parallel", "parallel"),
        name="fourier_channel_dft",
    )(proj, cs)


def _seq_dft_kernel(a_ref, pq_ref, z_ref, y_ref, acc_ref):
    k = pl.program_id(2)

    @pl.when(k == 0)
    def _():
        acc_ref[...] = jnp.zeros_like(acc_ref)

    acc_ref[...] += _dot(a_ref[...], pq_ref[0, 0])

    @pl.when(k == pl.num_programs(2) - 1)
    def _():
        y_ref[0] = (acc_ref[...] * _silu(z_ref[0].astype(F32))).astype(y_ref.dtype)


def _seq_dft_latent(dft_lat, pq, proj):
    bsz = proj.shape[0]
    k_per_part = N_LAT // DFT_TK
    return pl.pallas_call(
        _seq_dft_kernel,
        grid=(bsz, N_LAT // DFT_TM, 2 * k_per_part),
        in_specs=[
            pl.BlockSpec((DFT_TM, DFT_TK), lambda b, i, k: (i, k)),
            pl.BlockSpec((1, 1, DFT_TK, D_MODEL),
                         lambda b, i, k: (b, k // k_per_part, k % k_per_part, 0)),
            pl.BlockSpec((1, DFT_TM, D_MODEL), lambda b, i, k: (b, i, COL_CZ)),
        ],
        out_specs=pl.BlockSpec((1, DFT_TM, D_MODEL), lambda b, i, k: (b, i, 0)),
        out_shape=jax.ShapeDtypeStruct((bsz, N_TOK, D_MODEL), BF16),
        scratch_shapes=[pltpu.VMEM((DFT_TM, D_MODEL), F32)],
        compiler_params=_cparams("parallel", "parallel", "arbitrary"),
        name="fourier_token_dft_latent",
    )(dft_lat, pq, proj)


def _seq_dft_ctx_kernel(a_ref, pq_ref, z_ref, y_in_ref, y_ref):
    del y_in_ref
    r = _dot(a_ref[:, :N_CTX], pq_ref[0, 0]) + _dot(a_ref[:, N_CTX:], pq_ref[0, 1])
    y_ref[0] = (r * _silu(z_ref[0].astype(F32))).astype(y_ref.dtype)


def _seq_dft_ctx(dft_ctx, pq, proj, y_c):
    bsz = proj.shape[0]
    ctx_blk = N_LAT // N_CTX
    return pl.pallas_call(
        _seq_dft_ctx_kernel,
        grid=(bsz,),
        in_specs=[
            pl.BlockSpec((N_CTX, 2 * N_CTX), lambda b: (0, 0)),
            pl.BlockSpec((1, 2, N_CTX, D_MODEL), lambda b: (b, 0, ctx_blk, 0)),
            pl.BlockSpec((1, N_CTX, D_MODEL), lambda b: (b, ctx_blk, COL_CZ)),
            pl.BlockSpec(memory_space=pl.ANY),
        ],
        out_specs=pl.BlockSpec((1, N_CTX, D_MODEL), lambda b: (b, ctx_blk, 0)),
        out_shape=jax.ShapeDtypeStruct(y_c.shape, y_c.dtype),
        input_output_aliases={3: 0},
        compiler_params=_cparams("parallel"),
        name="fourier_token_dft_ctx",
    )(dft_ctx, pq, proj, y_c)


def _merge_kernel(x_ref, ya_ref, yb_ref, yc_ref, ga_ref, gb_ref, gc_ref, mod_ref,
                  wa_ref, wb_ref, wc_ref, wo_ref, o_ref, *, tile):
    b = pl.program_id(0)

    def branch(y_ref, g_ref, w_ref):
        return _sigmoid(g_ref[0].astype(F32)) * _dot(y_ref[0], w_ref[...])

    y = branch(ya_ref, ga_ref, wa_ref) + branch(yb_ref, gb_ref, wb_ref) + branch(yc_ref, gc_ref, wc_ref)
    out = _dot(y.astype(BF16), wo_ref[...])
    gate_cols = slice(2 * D_MODEL, 3 * D_MODEL)
    gate = jnp.where(_ctx_rows(pl.program_id(1), tile),
                     mod_ref[pl.ds(B_CTX_ROW, 1), gate_cols], mod_ref[pl.ds(b, 1), gate_cols])
    o_ref[0] = x_ref[0] + gate * out


def _merge(xs, y_a, y_b, y_c, proj, mod_l, wa, wb, wc, wo, n_rows, tile):
    bsz = xs.shape[0]
    col = lambda c: pl.BlockSpec((1, tile, D_MODEL), lambda b, t: (b, t, c))
    weight = pl.BlockSpec((D_MODEL, D_MODEL), lambda b, t: (0, 0))
    return pl.pallas_call(
        functools.partial(_merge_kernel, tile=tile),
        grid=(bsz, n_rows // tile),
        in_specs=[col(0), col(0), col(0), col(0), col(COL_MG), col(COL_MG + 1), col(COL_MG + 2),
                  pl.BlockSpec((SUBLANES, 3 * D_MODEL), lambda b, t: (0, 0)),
                  weight, weight, weight, weight],
        out_specs=col(0),
        out_shape=jax.ShapeDtypeStruct((bsz, n_rows, D_MODEL), F32),
        compiler_params=_cparams("parallel", "parallel"),
        name="merge_out_residual",
    )(xs, y_a, y_b, y_c, proj, proj, proj, mod_l, wa, wb, wc, wo)


def _rope_tables():
    n = jnp.arange(N_LAT, dtype=jnp.int32)
    row = (n // GRID_W).astype(F32)
    col = (n % GRID_W).astype(F32)
    inv_freq = ROPE_THETA ** (-jnp.arange(0, ROPE_AXIS_DIM, 2, dtype=F32) / ROPE_AXIS_DIM)
    ang_r = row[:, None] * inv_freq
    ang_c = col[:, None] * inv_freq
    cos = jnp.concatenate([jnp.cos(ang_r)] * 2 + [jnp.cos(ang_c)] * 2, axis=-1)
    sin = jnp.concatenate([-jnp.sin(ang_r), jnp.sin(ang_r), -jnp.sin(ang_c), jnp.sin(ang_c)], axis=-1)
    cos = jnp.concatenate([cos, jnp.ones((N_CTX, B_QK), F32)], axis=0)
    sin = jnp.concatenate([sin, jnp.zeros((N_CTX, B_QK), F32)], axis=0)
    reps = LANES // B_QK
    return jnp.tile(cos, (1, reps)), jnp.tile(sin, (1, reps))


def _dft_cos_sin(n, scale):
    idx = jnp.arange(n, dtype=jnp.int32)
    ang = ((idx[:, None] * idx[None, :]) % n).astype(F32) * (2.0 * math.pi / n)
    return jnp.cos(ang) * scale, jnp.sin(ang) * scale


def _dft_tables():
    c_ch, s_ch = _dft_cos_sin(C_GROUP_DIM, 1.0)
    chan = jnp.concatenate([c_ch, s_ch], axis=1).astype(BF16)
    c_l, s_l = _dft_cos_sin(N_LAT, (N_LAT * C_GROUP_DIM) ** -0.5)
    lat = jnp.concatenate([c_l, -s_l], axis=1).astype(BF16)
    c_c, s_c = _dft_cos_sin(N_CTX, (N_CTX * C_GROUP_DIM) ** -0.5)
    ctx = jnp.concatenate([c_c, -s_c], axis=1).astype(BF16)
    return chan, lat, ctx


def _group_sum_matrix():
    g = np.arange(LANES) // B_QK
    return jnp.asarray(g[:, None] == g[None, :], dtype=BF16)


def kernel(x, c, ctx, c_ctx, norm_w, w_ada, b_ada, w_in, b_if, a_norm_w, q_norm_w, k_norm_w,
           lambda_q1, lambda_k1, lambda_q2, lambda_k2, subln_w, w_a_out, w_b_out, w_c_out, w_out):
    bsz = x.shape[0]
    xs = jnp.concatenate([x, ctx], axis=1)
    c_rows = jnp.concatenate(
        [c, c_ctx[None, :], jnp.zeros((SUBLANES - bsz - 1, D_MODEL), F32)], axis=0)
    mod, lam = _modulation(c_rows, w_ada, b_ada, lambda_q1, lambda_k1, lambda_q2, lambda_k2)

    cos_t, sin_t = _rope_tables()
    dft_chan, dft_lat, dft_ctx = _dft_tables()
    group_sum = _group_sum_matrix()

    lane_pad = LANES - N_GATE
    w_proj = jnp.concatenate(
        [w_in[:, :, :GATE_COL0], w_in[:, :, GATE_COL0 + N_GATE:]], axis=2).astype(BF16)
    w_gate = jnp.pad(w_in[:, :, GATE_COL0:GATE_COL0 + N_GATE], ((0, 0), (0, 0), (0, lane_pad)))
    b_gate = jnp.pad(b_if, ((0, 0), (0, lane_pad)))
    wa, wb, wc, wo = (w.astype(BF16) for w in (w_a_out, w_b_out, w_c_out, w_out))
    tile2 = lambda w: jnp.tile(w[None, :], (1, LANES // B_QK))

    for l in range(DEPTH):
        last = l == DEPTH - 1
        h, gates = _norm_mod(xs, norm_w[l][None, :], mod[l], w_gate[l], b_gate[l][None, :])
        proj = _in_projection(h.reshape(bsz * N_TOK, D_MODEL), w_proj[l]).reshape(bsz, N_TOK, N_PROJ)

        gates_t = jnp.swapaxes(gates[:, :, :N_GATE], 1, 2)
        h_fwd, h_bwd = _mlstm(proj, gates, gates_t)
        y_a = _mlstm_out(h_fwd, h_bwd, proj, a_norm_w[l][None, :])

        qh, kt = _qk_prep(proj, cos_t, sin_t, tile2(q_norm_w[l]), tile2(k_norm_w[l]), group_sum)
        lam_l = lam[l, 0, :1]
        lam_init = 0.8 - 0.6 * math.exp(-0.3 * l)
        y_b = _diff_attention(lam_l, qh, kt, proj, subln_w[l][None, :], lam_init)

        pq = _chan_dft(proj, dft_chan)
        y_c = _seq_dft_latent(dft_lat, pq, proj)

        if last:
            return _merge(xs, y_a, y_b, y_c, proj, mod[l], wa[l], wb[l], wc[l], wo[l],
                          n_rows=N_LAT, tile=LAST_TILE)
        y_b = _diff_attention(lam_l, qh, kt, proj, subln_w[l][None, :], lam_init, y_latent=y_b)
        y_c = _seq_dft_ctx(dft_ctx, pq, proj, y_c)
        xs = _merge(xs, y_a, y_b, y_c, proj, mod[l], wa[l], wb[l], wc[l], wo[l],
                    n_rows=N_TOK, tile=TOK_TILE)
```

```python
import functools
import math

import jax
import jax.numpy as jnp
import numpy as np
from jax import lax
from jax.experimental import pallas as pl
from jax.experimental.pallas import tpu as pltpu

F32 = jnp.float32
BF16 = jnp.bfloat16
LOG2E = math.log2(math.e)

D_MODEL = 1024
DEPTH = 4
N_LAT = 4096
N_CTX = 256
N_TOK = N_LAT + N_CTX
GRID_W = 64
EPS = 1e-6

A_HEADS, A_QK, A_V = 4, 128, 256
B_HEADS, B_QK, B_V = 8, 64, 128
C_GROUPS, C_GROUP_DIM = 4, 256
ROPE_AXIS_DIM = B_QK // 2
ROPE_THETA = 10000.0
N_GATE = 4 * A_HEADS
BATCH = 4
B_CTX_ROW = BATCH

COL_AQK, COL_AV, COL_AO, COL_AZ, COL_BQ, COL_BK, COL_BV, COL_BZ, COL_CU, COL_CZ, COL_MG = (
    0, 1, 2, 3, 4, 5, 6, 7, 8, 9, 10)
N_PROJ = 13 * D_MODEL
GATE_COL0 = 2 * A_HEADS * A_QK + A_HEADS * A_V

LANES = 128
SUBLANES = 8
VMEM_LIMIT_BYTES = 56 * 1024 * 1024

TOK_TILE = 544
MM_TM = 2176
MM_TN = 1024
ATT_TQ = 256
ATT_ROWS = 2 * ATT_TQ
QK_TILE = 256
LAST_TILE = 512
MLSTM_CHUNK = 256
DFT_TM = 1024
DFT_TK = 2048


def _cparams(*sem):
    return pltpu.CompilerParams(dimension_semantics=sem, vmem_limit_bytes=VMEM_LIMIT_BYTES)


def _split3(x):
    hi = x.astype(BF16)
    r1 = x - hi.astype(F32)
    mid = r1.astype(BF16)
    lo = (r1 - mid.astype(F32)).astype(BF16)
    return hi, mid, lo


def _dot(a, b):
    return jnp.dot(a, b, preferred_element_type=F32)


def _dot_f32_rhs(a_exact, b):
    hi, mid, lo = _split3(b)
    return _dot(a_exact, hi) + _dot(a_exact, mid) + _dot(a_exact, lo)


def _dot_f32_lhs(a, b_exact):
    hi, mid, lo = _split3(a)
    return _dot(hi, b_exact) + _dot(mid, b_exact) + _dot(lo, b_exact)


def _dot_f32(a, b):
    a_hi = a.astype(BF16)
    a_lo = (a - a_hi.astype(F32)).astype(BF16)
    b_hi = b.astype(BF16)
    b_lo = (b - b_hi.astype(F32)).astype(BF16)
    return _dot(a_hi, b_hi) + _dot(a_lo, b_hi) + _dot(a_hi, b_lo)


def _sigmoid(x):
    return 1.0 / (1.0 + jnp.exp(-x))


def _silu(x):
    return x * _sigmoid(x)


def _log_sigmoid(x):
    return jnp.minimum(x, 0.0) - jnp.log(1.0 + jnp.exp(-jnp.abs(x)))


def _mod_kernel(c_ref, w_ref, b_ref, q1_ref, k1_ref, q2_ref, k2_ref, mod_ref, lam_ref):
    sc = _silu(c_ref[...])
    mod_ref[0] = _dot_f32(sc, w_ref[0]) + b_ref[0]
    layer = jnp.full((1, LANES), pl.program_id(0), dtype=jnp.int32).astype(F32)
    lam_init = 0.8 - 0.6 * jnp.exp(-0.3 * layer)
    s1 = jnp.sum(q1_ref[0] * k1_ref[0], axis=-1, keepdims=True)
    s2 = jnp.sum(q2_ref[0] * k2_ref[0], axis=-1, keepdims=True)
    lam_ref[0] = jnp.exp(s1) - jnp.exp(s2) + lam_init


def _modulation(c_rows, w_ada, b_ada, lq1, lk1, lq2, lk2):
    n_col = 3
    vec = lambda a: a.reshape(DEPTH, 1, B_QK)
    lam_spec = pl.BlockSpec((1, 1, B_QK), lambda l, j: (l, 0, 0))
    return pl.pallas_call(
        _mod_kernel,
        grid=(DEPTH, n_col),
        in_specs=[
            pl.BlockSpec((SUBLANES, D_MODEL), lambda l, j: (0, 0)),
            pl.BlockSpec((1, D_MODEL, D_MODEL), lambda l, j: (l, 0, j)),
            pl.BlockSpec((1, 1, D_MODEL), lambda l, j: (l, 0, j)),
            lam_spec, lam_spec, lam_spec, lam_spec,
        ],
        out_specs=[
            pl.BlockSpec((1, SUBLANES, D_MODEL), lambda l, j: (l, 0, j)),
            pl.BlockSpec((1, 1, LANES), lambda l, j: (l, 0, 0)),
        ],
        out_shape=[
            jax.ShapeDtypeStruct((DEPTH, SUBLANES, 3 * D_MODEL), F32),
            jax.ShapeDtypeStruct((DEPTH, 1, LANES), F32),
        ],
        compiler_params=_cparams("arbitrary", "arbitrary"),
        name="adaln_modulation",
    )(c_rows, w_ada, b_ada.reshape(DEPTH, 1, 3 * D_MODEL), vec(lq1), vec(lk1), vec(lq2), vec(lk2))


def _ctx_rows(tile_idx, tile):
    row = tile_idx * tile + lax.broadcasted_iota(jnp.int32, (tile, 1), 0)
    return row >= N_LAT


def _norm_mod_kernel(x_ref, nw_ref, mod_ref, wg_ref, bif_ref, h_ref, g_ref):
    b = pl.program_id(0)
    x = x_ref[0]
    y = x * lax.rsqrt(jnp.mean(x * x, axis=-1, keepdims=True) + EPS) * nw_ref[...]
    is_ctx = _ctx_rows(pl.program_id(1), TOK_TILE)
    m_lat = mod_ref[pl.ds(b, 1), :]
    m_ctx = mod_ref[pl.ds(B_CTX_ROW, 1), :]
    shift = jnp.where(is_ctx, m_ctx[:, 0:D_MODEL], m_lat[:, 0:D_MODEL])
    scale = jnp.where(is_ctx, m_ctx[:, D_MODEL:2 * D_MODEL], m_lat[:, D_MODEL:2 * D_MODEL])
    h = y * (1.0 + scale) + shift
    h_ref[0] = h.astype(BF16)
    g_ref[0] = _dot_f32(h, wg_ref[...]) + bif_ref[...]


def _norm_mod(xs, norm_w, mod_l, w_gate, b_if):
    bsz = xs.shape[0]
    tok = pl.BlockSpec((1, TOK_TILE, D_MODEL), lambda b, t: (b, t, 0))
    full = lambda shape: pl.BlockSpec(shape, lambda b, t: (0,) * len(shape))
    return pl.pallas_call(
        _norm_mod_kernel,
        grid=(bsz, N_TOK // TOK_TILE),
        in_specs=[tok, full((1, D_MODEL)), full((SUBLANES, 3 * D_MODEL)),
                  full((D_MODEL, LANES)), full((1, LANES))],
        out_specs=[tok, pl.BlockSpec((1, TOK_TILE, LANES), lambda b, t: (b, t, 0))],
        out_shape=[jax.ShapeDtypeStruct((bsz, N_TOK, D_MODEL), BF16),
                   jax.ShapeDtypeStruct((bsz, N_TOK, LANES), F32)],
        compiler_params=_cparams("parallel", "parallel"),
        name="norm_modulate",
    )(xs, norm_w, mod_l, w_gate, b_if)


def _in_proj_kernel(a_ref, w0_ref, w1_ref, tail_ref, o_ref, wb_ref):
    j = pl.program_id(0)
    n_plain = GATE_COL0 // MM_TN

    @pl.when((pl.program_id(1) == 0) & (j < n_plain))
    def _():
        wb_ref[...] = w0_ref[0].astype(BF16)

    @pl.when((pl.program_id(1) == 0) & (j >= n_plain))
    def _():
        shifted = pltpu.roll(w0_ref[0], MM_TN - N_GATE, 1)
        nxt = jnp.where(j == pl.num_programs(0) - 1, tail_ref[0], w1_ref[0, :, :LANES])
        lane = lax.broadcasted_iota(jnp.int32, (1, LANES), 1)
        last = jnp.where(lane < LANES - N_GATE, shifted[:, MM_TN - LANES:],
                         pltpu.roll(nxt, LANES - N_GATE, 1))
        wb_ref[...] = jnp.concatenate([shifted[:, :MM_TN - LANES], last], axis=1).astype(BF16)

    o_ref[...] = _dot(a_ref[...], wb_ref[...]).astype(o_ref.dtype)


def _in_projection(h2d, w_in, w_tail, layer):
    m, k = h2d.shape
    n_col = N_PROJ // MM_TN
    return pl.pallas_call(
        _in_proj_kernel,
        grid=(n_col, m // MM_TM),
        in_specs=[pl.BlockSpec((MM_TM, k), lambda j, i: (i, 0)),
                  pl.BlockSpec((1, k, MM_TN), lambda j, i: (layer, 0, j)),
                  pl.BlockSpec((1, k, MM_TN), lambda j, i: (layer, 0, jnp.minimum(j + 1, n_col - 1))),
                  pl.BlockSpec((1, k, LANES), lambda j, i: (layer, 0, 0))],
        out_specs=pl.BlockSpec((MM_TM, MM_TN), lambda j, i: (i, j)),
        out_shape=jax.ShapeDtypeStruct((m, N_PROJ), BF16),
        scratch_shapes=[pltpu.VMEM((k, MM_TN), BF16)],
        compiler_params=_cparams("arbitrary", "arbitrary"),
        name="in_projection",
    )(h2d, w_in, w_in, w_tail)


def _qk_prep_kernel(q_ref, k_ref, cos_ref, sin_ref, qw_ref, kw_ref, g_ref, qo_ref, kt_ref):
    cos = cos_ref[...]
    sin = sin_ref[...]
    group_sum = g_ref[...]
    lane = lax.broadcasted_iota(jnp.int32, (1, LANES), 1)
    first_half = (lane % ROPE_AXIS_DIM) < (ROPE_AXIS_DIM // 2)

    def prep(x_bf16, w):
        x = x_bf16.astype(F32)
        ms = _dot_f32_lhs(x * x, group_sum) * (1.0 / B_QK)
        y = x * lax.rsqrt(ms + EPS) * w
        half = ROPE_AXIS_DIM // 2
        partner = jnp.where(first_half, pltpu.roll(y, LANES - half, 1), pltpu.roll(y, half, 1))
        return y * cos + partner * sin

    for s in range(D_MODEL // LANES):
        cols = slice(s * LANES, (s + 1) * LANES)
        qo_ref[0, :, cols] = (prep(q_ref[0, :, cols], qw_ref[...]) * (B_QK ** -0.5 * LOG2E)).astype(BF16)
        kt_ref[0, cols, :] = prep(k_ref[0, :, cols], kw_ref[...]).T.astype(BF16)


def _qk_prep(proj, cos_t, sin_t, qw, kw, group_sum):
    bsz = proj.shape[0]
    col = lambda c: pl.BlockSpec((1, QK_TILE, D_MODEL), lambda b, t: (b, t, c))
    tab = pl.BlockSpec((QK_TILE, LANES), lambda b, t: (t, 0))
    full = lambda shape: pl.BlockSpec(shape, lambda b, t: (0,) * len(shape))
    return pl.pallas_call(
        _qk_prep_kernel,
        grid=(bsz, N_TOK // QK_TILE),
        in_specs=[col(COL_BQ), col(COL_BK), tab, tab, full((1, LANES)), full((1, LANES)),
                  full((LANES, LANES))],
        out_specs=[pl.BlockSpec((1, QK_TILE, D_MODEL), lambda b, t: (b, t, 0)),
                   pl.BlockSpec((1, D_MODEL, QK_TILE), lambda b, t: (b, 0, t))],
        out_shape=[jax.ShapeDtypeStruct((bsz, N_TOK, D_MODEL), BF16),
                   jax.ShapeDtypeStruct((bsz, D_MODEL, N_TOK), BF16)],
        compiler_params=_cparams("parallel", "parallel"),
        name="qk_norm_rope",
    )(proj, proj, cos_t, sin_t, qw, kw, group_sum)


def _diff_attn_body(lam_ref, q_ref, kt_ref, v_ref, z_ref, sw_ref, o_ref, vp_ref, lam_init):
    @pl.when(pl.program_id(2) == 0)
    def _():
        vp_ref[:, :B_V] = v_ref[0]
        vp_ref[:, B_V:] = jnp.ones((vp_ref.shape[0], B_V), BF16)

    chains = [slice(c * ATT_TQ, (c + 1) * ATT_TQ) for c in range(q_ref.shape[1] // ATT_TQ)]
    lane = lax.broadcasted_iota(jnp.int32, (ATT_TQ, B_V), 1)
    zero = jnp.zeros((ATT_TQ, B_V), BF16)

    def scores(rows, first_map):
        q = q_ref[0, rows, :]
        return _dot(jnp.where((lane < B_QK) if first_map else (lane >= B_QK), q, zero), kt_ref[0])

    def numerators(s):
        return jnp.exp2(s - jnp.max(s, axis=-1, keepdims=True)).astype(BF16)

    s0 = [scores(rows, True) for rows in chains]
    s1 = [scores(rows, False) for rows in chains]
    u0 = [_dot(numerators(s), vp_ref[...]) for s in s0]
    u1 = [_dot(numerators(s), vp_ref[...]) for s in s1]
    for rows, a, b in zip(chains, u0, u1):
        o = a[:, :B_V] / a[:, B_V:] - lam_ref[0] * (b[:, :B_V] / b[:, B_V:])
        y = o * lax.rsqrt(jnp.mean(o * o, axis=-1, keepdims=True) + EPS) * sw_ref[...]
        o_ref[0, rows, :] = (y * (1.0 - lam_init)
                             * _silu(z_ref[0, rows, :].astype(F32))).astype(o_ref.dtype)


def _diff_attn_kernel(lam_ref, q_ref, kt_ref, v_ref, z_ref, sw_ref, o_ref, vp_ref, *, lam_init):
    _diff_attn_body(lam_ref, q_ref, kt_ref, v_ref, z_ref, sw_ref, o_ref, vp_ref, lam_init)


def _diff_attn_ctx_kernel(lam_ref, q_ref, kt_ref, v_ref, z_ref, sw_ref, y_in_ref, o_ref, vp_ref,
                          *, lam_init):
    del y_in_ref
    _diff_attn_body(lam_ref, q_ref, kt_ref, v_ref, z_ref, sw_ref, o_ref, vp_ref, lam_init)


def _diff_attention(lam, qh, kt, proj, subln_w, lam_init, y_latent=None):
    bsz = qh.shape[0]
    heads_per_block = D_MODEL // B_V
    ctx = y_latent is not None
    n_keys = N_CTX if ctx else N_TOK
    rows = N_CTX if ctx else ATT_ROWS
    q0 = N_LAT // rows if ctx else 0
    k0 = N_LAT // N_CTX if ctx else 0
    tile = lambda c0: pl.BlockSpec((1, rows, B_V), lambda b, h, i: (b, q0 + i, c0 + h))
    in_specs = [pl.BlockSpec(memory_space=pltpu.SMEM),
                tile(0),
                pl.BlockSpec((1, B_V, n_keys), lambda b, h, i: (b, h, k0)),
                pl.BlockSpec((1, n_keys, B_V), lambda b, h, i: (b, k0, COL_BV * heads_per_block + h)),
                tile(COL_BZ * heads_per_block),
                pl.BlockSpec((1, B_V), lambda b, h, i: (0, 0))]
    args = [lam, qh, kt, proj, proj, subln_w]
    if ctx:
        in_specs.append(pl.BlockSpec(memory_space=pl.ANY))
        args.append(y_latent)
    return pl.pallas_call(
        functools.partial(_diff_attn_ctx_kernel if ctx else _diff_attn_kernel, lam_init=lam_init),
        grid=(bsz, B_HEADS, (N_CTX if ctx else N_LAT) // rows),
        in_specs=in_specs,
        out_specs=tile(0),
        out_shape=jax.ShapeDtypeStruct((bsz, N_TOK, D_MODEL), BF16),
        scratch_shapes=[pltpu.VMEM((n_keys, 2 * B_V), BF16)],
        input_output_aliases={6: 0} if ctx else {},
        compiler_params=_cparams("parallel", "parallel", "arbitrary"),
        name="diff_attention_ctx" if ctx else "diff_attention",
    )(*args)


def _mlstm_gates(gc_ref, gr_ref, reverse):
    chunk = MLSTM_CHUNK
    row = lax.broadcasted_iota(jnp.int32, (chunk, chunk), 0)
    col = lax.broadcasted_iota(jnp.int32, (chunk, chunk), 1)
    visible = (row <= col) if reverse else (row >= col)
    vis_ts = jnp.where(visible, 1.0, 0.0).astype(BF16)
    vis_st = jnp.where((row >= col) if reverse else (row <= col), 1.0, 0.0).astype(BF16)
    gate_c = gc_ref[0]
    gate_r = gr_ref[0]
    logf_r = _log_sigmoid(gate_r)
    return dict(visible=visible, gate_c=gate_c, gate_r=gate_r, logf_r=logf_r,
                cum_c=_dot_f32_rhs(vis_ts, _log_sigmoid(gate_c)),
                cum_r=_dot_f32_lhs(logf_r, vis_st))


def _mlstm_scores(g, h, reverse, q_ref, k_ref, m_ref):
    d = 1 if reverse else 0
    i_idx = (2 * d) * A_HEADS + h
    f_idx = (2 * d + 1) * A_HEADS + h
    b_c = g["cum_c"][:, f_idx:f_idx + 1]
    ic_r = g["gate_r"][i_idx:i_idx + 1, :]
    b_r = g["cum_r"][f_idx:f_idx + 1, :]
    m_st = m_ref[h, 0:1, 0:1]
    qh = q_ref[0, :, h * A_QK:(h + 1) * A_QK]
    ks = k_ref[0, :, h * A_QK:(h + 1) * A_QK].astype(F32) * (A_QK ** -0.5)
    dmat = jnp.where(g["visible"], b_c - b_r + ic_r, -jnp.inf)
    inter = b_c + m_st
    m_row = jnp.maximum(jnp.max(dmat, axis=1, keepdims=True), inter)
    s = lax.dot_general(qh, ks.astype(BF16), (((1,), (1,)), ((), ())), preferred_element_type=F32)
    return dict(
        qh=qh, ks=ks, m_st=m_st, b_c=b_c, m_row=m_row,
        ic_c=g["gate_c"][:, i_idx:i_idx + 1],
        b_last=jnp.sum(g["logf_r"][f_idx:f_idx + 1, :], axis=1, keepdims=True),
        a=jnp.exp(dmat - m_row) * s, w_inter=jnp.exp(inter - m_row))


def _mlstm_output(t, h, v_ref, o_ref, c_ref, n_ref):
    vh = v_ref[0, :, h * A_V:(h + 1) * A_V]
    n_st = n_ref[h, 0:1, :]
    num = _dot(t["a"].astype(BF16), vh) + t["w_inter"] * _dot(t["qh"], c_ref[h].astype(BF16))
    den = (jnp.sum(t["a"], axis=1, keepdims=True)
           + t["w_inter"] * jnp.sum(t["qh"].astype(F32) * n_st, axis=1, keepdims=True))
    o_ref[0, :, h * A_V:(h + 1) * A_V] = num / jnp.maximum(jnp.abs(den), jnp.exp(-t["m_row"]))


def _mlstm_state(t, h, v_ref, c_ref, n_ref, m_ref):
    vh = v_ref[0, :, h * A_V:(h + 1) * A_V]
    g = t["b_last"] - t["b_c"] + t["ic_c"]
    m_new = jnp.maximum(t["b_last"] + t["m_st"], jnp.max(g, axis=0, keepdims=True))
    kw = t["ks"] * jnp.exp(g - m_new)
    decay = jnp.exp(t["b_last"] + t["m_st"] - m_new)
    c_ref[h] = decay * c_ref[h] + _dot(kw.T.astype(BF16), vh)
    n_ref[h] = jnp.broadcast_to(decay * n_ref[h, 0:1, :] + jnp.sum(kw, axis=0, keepdims=True),
                                (SUBLANES, A_QK))
    m_ref[h] = jnp.broadcast_to(m_new, (SUBLANES, LANES))


def _mlstm_kernel(*refs):
    n_in, n_state = 5, 3
    ins = (refs[:n_in], refs[n_in:2 * n_in])
    outs = (refs[2 * n_in], refs[2 * n_in + 1])
    state = refs[2 * n_in + 2:]
    states = (state[:n_state], state[n_state:])

    @pl.when(pl.program_id(1) == 0)
    def _():
        for ref in state:
            ref[...] = jnp.zeros_like(ref)

    gates = [_mlstm_gates(ins[d][3], ins[d][4], reverse=bool(d)) for d in range(2)]
    chains = [(d, h) for h in range(A_HEADS) for d in range(2)]
    terms = {}
    for d, h in chains:
        q_ref, k_ref = ins[d][0], ins[d][1]
        terms[d, h] = _mlstm_scores(gates[d], h, bool(d), q_ref, k_ref, states[d][2])
    for d, h in chains:
        c_ref, n_ref, _ = states[d]
        _mlstm_output(terms[d, h], h, ins[d][2], outs[d], c_ref, n_ref)
    for d, h in chains:
        c_ref, n_ref, m_ref = states[d]
        _mlstm_state(terms[d, h], h, ins[d][2], c_ref, n_ref, m_ref)


def _mlstm(proj, gates, gates_t):
    bsz = proj.shape[0]
    n_lat_chunks = N_LAT // MLSTM_CHUNK

    def chunk_of(step, reverse):
        latent = (n_lat_chunks - step) if reverse else (step - 1)
        return jnp.where(step == 0, n_lat_chunks, latent)

    def specs(reverse):
        ck = functools.partial(chunk_of, reverse=reverse)
        return [
            pl.BlockSpec((1, MLSTM_CHUNK, A_HEADS * A_QK), lambda b, s: (b, ck(s), 0)),
            pl.BlockSpec((1, MLSTM_CHUNK, A_HEADS * A_QK), lambda b, s: (b, ck(s), 1)),
            pl.BlockSpec((1, MLSTM_CHUNK, D_MODEL), lambda b, s: (b, ck(s), COL_AV)),
            pl.BlockSpec((1, MLSTM_CHUNK, LANES), lambda b, s: (b, ck(s), 0)),
            pl.BlockSpec((1, N_GATE, MLSTM_CHUNK), lambda b, s: (b, 0, ck(s))),
        ]

    def out_spec(reverse):
        ck = functools.partial(chunk_of, reverse=reverse)
        return pl.BlockSpec((1, MLSTM_CHUNK, D_MODEL), lambda b, s: (b, ck(s), 0))

    state = [pltpu.VMEM((A_HEADS, A_QK, A_V), F32),
             pltpu.VMEM((A_HEADS, SUBLANES, A_QK), F32),
             pltpu.VMEM((A_HEADS, SUBLANES, LANES), F32)]
    args = (proj, proj, proj, gates, gates_t)
    return pl.pallas_call(
        _mlstm_kernel,
        grid=(bsz, N_TOK // MLSTM_CHUNK),
        in_specs=specs(False) + specs(True),
        out_specs=[out_spec(False), out_spec(True)],
        out_shape=[jax.ShapeDtypeStruct((bsz, N_TOK, D_MODEL), F32)] * 2,
        scratch_shapes=state + state,
        compiler_params=_cparams("parallel", "arbitrary"),
        name="mlstm_bidirectional",
    )(*args, *args)


def _mlstm_out_kernel(hf_ref, hb_ref, o_ref, z_ref, w_ref, y_ref):
    for h in range(A_HEADS):
        cols = slice(h * A_V, (h + 1) * A_V)
        x = hf_ref[0, :, cols] + hb_ref[0, :, cols]
        xn = x * lax.rsqrt(jnp.mean(x * x, axis=-1, keepdims=True) + EPS) * w_ref[:, cols]
        gate = _sigmoid(o_ref[0, :, cols].astype(F32)) * _silu(z_ref[0, :, cols].astype(F32))
        y_ref[0, :, cols] = (xn * gate).astype(y_ref.dtype)


def _mlstm_out(h_fwd, h_bwd, proj, a_norm_w):
    bsz = proj.shape[0]
    col = lambda c: pl.BlockSpec((1, TOK_TILE, D_MODEL), lambda b, t: (b, t, c))
    return pl.pallas_call(
        _mlstm_out_kernel,
        grid=(bsz, N_TOK // TOK_TILE),
        in_specs=[col(0), col(0), col(COL_AO), col(COL_AZ),
                  pl.BlockSpec((1, D_MODEL), lambda b, t: (0, 0))],
        out_specs=col(0),
        out_shape=jax.ShapeDtypeStruct((bsz, N_TOK, D_MODEL), BF16),
        compiler_params=_cparams("parallel", "parallel"),
        name="mlstm_out",
    )(h_fwd, h_bwd, proj, proj, a_norm_w)


def _chan_dft_kernel(u_ref, cs_ref, pq_ref):
    for g in range(C_GROUPS):
        cols = slice(g * C_GROUP_DIM, (g + 1) * C_GROUP_DIM)
        r = _dot(u_ref[0, :, cols], cs_ref[...])
        pq_ref[0, 0, :, cols] = r[:, :C_GROUP_DIM].astype(pq_ref.dtype)
        pq_ref[0, 1, :, cols] = r[:, C_GROUP_DIM:].astype(pq_ref.dtype)


def _chan_dft(proj, cs):
    bsz = proj.shape[0]
    return pl.pallas_call(
        _chan_dft_kernel,
        grid=(bsz, N_TOK // TOK_TILE),
        in_specs=[pl.BlockSpec((1, TOK_TILE, D_MODEL), lambda b, t: (b, t, COL_CU)),
                  pl.BlockSpec((C_GROUP_DIM, 2 * C_GROUP_DIM), lambda b, t: (0, 0))],
        out_specs=pl.BlockSpec((1, 2, TOK_TILE, D_MODEL), lambda b, t: (b, 0, t, 0)),
        out_shape=jax.ShapeDtypeStruct((bsz, 2, N_TOK, D_MODEL), BF16),
        compiler_params=_cparams("parallel", "parallel"),
        name="fourier_channel_dft",
    )(proj, cs)


def _seq_dft_kernel(a_ref, pq_ref, z_ref, y_ref, acc_ref):
    k = pl.program_id(2)

    @pl.when(k == 0)
    def _():
        acc_ref[...] = jnp.zeros_like(acc_ref)

    acc_ref[...] += _dot(a_ref[...], pq_ref[0, 0])

    @pl.when(k == pl.num_programs(2) - 1)
    def _():
        y_ref[0] = (acc_ref[...] * _silu(z_ref[0].astype(F32))).astype(y_ref.dtype)


def _seq_dft_latent(dft_lat, pq, proj):
    bsz = proj.shape[0]
    k_per_part = N_LAT // DFT_TK
    return pl.pallas_call(
        _seq_dft_kernel,
        grid=(bsz, N_LAT // DFT_TM, 2 * k_per_part),
        in_specs=[
            pl.BlockSpec((DFT_TM, DFT_TK), lambda b, i, k: (i, k)),
            pl.BlockSpec((1, 1, DFT_TK, D_MODEL),
                         lambda b, i, k: (b, k // k_per_part, k % k_per_part, 0)),
            pl.BlockSpec((1, DFT_TM, D_MODEL), lambda b, i, k: (b, i, COL_CZ)),
        ],
        out_specs=pl.BlockSpec((1, DFT_TM, D_MODEL), lambda b, i, k: (b, i, 0)),
        out_shape=jax.ShapeDtypeStruct((bsz, N_TOK, D_MODEL), BF16),
        scratch_shapes=[pltpu.VMEM((DFT_TM, D_MODEL), F32)],
        compiler_params=_cparams("parallel", "parallel", "arbitrary"),
        name="fourier_token_dft_latent",
    )(dft_lat, pq, proj)


def _seq_dft_ctx_kernel(a_ref, pq_ref, z_ref, y_in_ref, y_ref):
    del y_in_ref
    r = _dot(a_ref[:, :N_CTX], pq_ref[0, 0]) + _dot(a_ref[:, N_CTX:], pq_ref[0, 1])
    y_ref[0] = (r * _silu(z_ref[0].astype(F32))).astype(y_ref.dtype)


def _seq_dft_ctx(dft_ctx, pq, proj, y_c):
    bsz = proj.shape[0]
    ctx_blk = N_LAT // N_CTX
    return pl.pallas_call(
        _seq_dft_ctx_kernel,
        grid=(bsz,),
        in_specs=[
            pl.BlockSpec((N_CTX, 2 * N_CTX), lambda b: (0, 0)),
            pl.BlockSpec((1, 2, N_CTX, D_MODEL), lambda b: (b, 0, ctx_blk, 0)),
            pl.BlockSpec((1, N_CTX, D_MODEL), lambda b: (b, ctx_blk, COL_CZ)),
            pl.BlockSpec(memory_space=pl.ANY),
        ],
        out_specs=pl.BlockSpec((1, N_CTX, D_MODEL), lambda b: (b, ctx_blk, 0)),
        out_shape=jax.ShapeDtypeStruct(y_c.shape, y_c.dtype),
        input_output_aliases={3: 0},
        compiler_params=_cparams("parallel"),
        name="fourier_token_dft_ctx",
    )(dft_ctx, pq, proj, y_c)


def _merge_kernel(x_ref, ya_ref, yb_ref, yc_ref, ga_ref, gb_ref, gc_ref, mod_ref,
                  wa_ref, wb_ref, wc_ref, wo_ref, o_ref, *, tile):
    b = pl.program_id(0)

    def branch(y_ref, g_ref, w_ref):
        return _sigmoid(g_ref[0].astype(F32)) * _dot(y_ref[0], w_ref[...])

    y = branch(ya_ref, ga_ref, wa_ref) + branch(yb_ref, gb_ref, wb_ref) + branch(yc_ref, gc_ref, wc_ref)
    out = _dot(y.astype(BF16), wo_ref[...])
    gate_cols = slice(2 * D_MODEL, 3 * D_MODEL)
    gate = jnp.where(_ctx_rows(pl.program_id(1), tile),
                     mod_ref[pl.ds(B_CTX_ROW, 1), gate_cols], mod_ref[pl.ds(b, 1), gate_cols])
    o_ref[0] = x_ref[0] + gate * out


def _merge(xs, y_a, y_b, y_c, proj, mod_l, wa, wb, wc, wo, n_rows, tile):
    bsz = xs.shape[0]
    col = lambda c: pl.BlockSpec((1, tile, D_MODEL), lambda b, t: (b, t, c))
    weight = pl.BlockSpec((D_MODEL, D_MODEL), lambda b, t: (0, 0))
    return pl.pallas_call(
        functools.partial(_merge_kernel, tile=tile),
        grid=(bsz, n_rows // tile),
        in_specs=[col(0), col(0), col(0), col(0), col(COL_MG), col(COL_MG + 1), col(COL_MG + 2),
                  pl.BlockSpec((SUBLANES, 3 * D_MODEL), lambda b, t: (0, 0)),
                  weight, weight, weight, weight],
        out_specs=col(0),
        out_shape=jax.ShapeDtypeStruct((bsz, n_rows, D_MODEL), F32),
        compiler_params=_cparams("parallel", "parallel"),
        name="merge_out_residual",
    )(xs, y_a, y_b, y_c, proj, proj, proj, mod_l, wa, wb, wc, wo)


def _rope_tables():
    n = jnp.arange(N_LAT, dtype=jnp.int32)
    row = (n // GRID_W).astype(F32)
    col = (n % GRID_W).astype(F32)
    inv_freq = ROPE_THETA ** (-jnp.arange(0, ROPE_AXIS_DIM, 2, dtype=F32) / ROPE_AXIS_DIM)
    ang_r = row[:, None] * inv_freq
    ang_c = col[:, None] * inv_freq
    cos = jnp.concatenate([jnp.cos(ang_r)] * 2 + [jnp.cos(ang_c)] * 2, axis=-1)
    sin = jnp.concatenate([-jnp.sin(ang_r), jnp.sin(ang_r), -jnp.sin(ang_c), jnp.sin(ang_c)], axis=-1)
    cos = jnp.concatenate([cos, jnp.ones((N_CTX, B_QK), F32)], axis=0)
    sin = jnp.concatenate([sin, jnp.zeros((N_CTX, B_QK), F32)], axis=0)
    reps = LANES // B_QK
    return jnp.tile(cos, (1, reps)), jnp.tile(sin, (1, reps))


def _dft_cos_sin(n, scale):
    idx = jnp.arange(n, dtype=jnp.int32)
    ang = ((idx[:, None] * idx[None, :]) % n).astype(F32) * (2.0 * math.pi / n)
    return jnp.cos(ang) * scale, jnp.sin(ang) * scale


def _dft_cos_sin_square(n, scale):
    r = math.isqrt(n)
    k = jnp.arange(n, dtype=jnp.int32)[:, None]
    j = jnp.arange(r, dtype=jnp.int32)[None, :]
    ang_a = ((k * j) % r).astype(F32) * (2.0 * math.pi / r)
    ang_b = ((k * j) % n).astype(F32) * (2.0 * math.pi / n)
    ca, sa = jnp.cos(ang_a)[:, :, None] * scale, jnp.sin(ang_a)[:, :, None] * scale
    cb, sb = jnp.cos(ang_b)[:, None, :], jnp.sin(ang_b)[:, None, :]
    return (ca * cb - sa * sb).reshape(n, n), (sa * cb + ca * sb).reshape(n, n)


def _dft_tables():
    c_ch, s_ch = _dft_cos_sin(C_GROUP_DIM, 1.0)
    chan = jnp.concatenate([c_ch, s_ch], axis=1).astype(BF16)
    c_l, s_l = _dft_cos_sin_square(N_LAT, (N_LAT * C_GROUP_DIM) ** -0.5)
    lat = jnp.concatenate([c_l, -s_l], axis=1).astype(BF16)
    c_c, s_c = _dft_cos_sin(N_CTX, (N_CTX * C_GROUP_DIM) ** -0.5)
    ctx = jnp.concatenate([c_c, -s_c], axis=1).astype(BF16)
    return chan, lat, ctx


def _group_sum_matrix():
    g = np.arange(LANES) // B_QK
    return jnp.asarray(g[:, None] == g[None, :], dtype=BF16)


def kernel(x, c, ctx, c_ctx, norm_w, w_ada, b_ada, w_in, b_if, a_norm_w, q_norm_w, k_norm_w,
           lambda_q1, lambda_k1, lambda_q2, lambda_k2, subln_w, w_a_out, w_b_out, w_c_out, w_out):
    bsz = x.shape[0]
    xs = jnp.concatenate([x, ctx], axis=1)
    c_rows = jnp.concatenate(
        [c, c_ctx[None, :], jnp.zeros((SUBLANES - bsz - 1, D_MODEL), F32)], axis=0)
    mod, lam = _modulation(c_rows, w_ada, b_ada, lambda_q1, lambda_k1, lambda_q2, lambda_k2)

    cos_t, sin_t = _rope_tables()
    dft_chan, dft_lat, dft_ctx = _dft_tables()
    group_sum = _group_sum_matrix()

    lane_pad = LANES - N_GATE
    pad_lanes = lambda w: jnp.pad(w, ((0, 0),) * (w.ndim - 1) + ((0, lane_pad),))
    w_gate = pad_lanes(w_in[:, :, GATE_COL0:GATE_COL0 + N_GATE])
    w_tail = pad_lanes(w_in[:, :, w_in.shape[2] - N_GATE:])
    b_gate = pad_lanes(b_if)
    wa, wb, wc, wo = (w.astype(BF16) for w in (w_a_out, w_b_out, w_c_out, w_out))
    tile2 = lambda w: jnp.tile(w[None, :], (1, LANES // B_QK))

    for l in range(DEPTH):
        last = l == DEPTH - 1
        h, gates = _norm_mod(xs, norm_w[l][None, :], mod[l], w_gate[l], b_gate[l][None, :])
        proj = _in_projection(h.reshape(bsz * N_TOK, D_MODEL), w_in, w_tail, l).reshape(bsz, N_TOK, N_PROJ)

        gates_t = jnp.swapaxes(gates[:, :, :N_GATE], 1, 2)
        h_fwd, h_bwd = _mlstm(proj, gates, gates_t)
        y_a = _mlstm_out(h_fwd, h_bwd, proj, a_norm_w[l][None, :])

        qh, kt = _qk_prep(proj, cos_t, sin_t, tile2(q_norm_w[l]), tile2(k_norm_w[l]), group_sum)
        lam_l = lam[l, 0, :1]
        lam_init = 0.8 - 0.6 * math.exp(-0.3 * l)
        y_b = _diff_attention(lam_l, qh, kt, proj, subln_w[l][None, :], lam_init)

        pq = _chan_dft(proj, dft_chan)
        y_c = _seq_dft_latent(dft_lat, pq, proj)

        if last:
            return _merge(xs, y_a, y_b, y_c, proj, mod[l], wa[l], wb[l], wc[l], wo[l],
                          n_rows=N_LAT, tile=LAST_TILE)
        y_b = _diff_attention(lam_l, qh, kt, proj, subln_w[l][None, :], lam_init, y_latent=y_b)
        y_c = _seq_dft_ctx(dft_ctx, pq, proj, y_c)
        xs = _merge(xs, y_a, y_b, y_c, proj, mod[l], wa[l], wb[l], wc[l], wo[l],
                    n_rows=N_TOK, tile=TOK_TILE)
```

```python
import functools
import math

import jax
import jax.numpy as jnp
import numpy as np
from jax import lax
from jax.experimental import pallas as pl
from jax.experimental.pallas import tpu as pltpu

F32 = jnp.float32
BF16 = jnp.bfloat16
LOG2E = math.log2(math.e)

D_MODEL = 1024
DEPTH = 4
N_LAT = 4096
N_CTX = 256
N_TOK = N_LAT + N_CTX
GRID_W = 64
EPS = 1e-6

A_HEADS, A_QK, A_V = 4, 128, 256
B_HEADS, B_QK, B_V = 8, 64, 128
C_GROUPS, C_GROUP_DIM = 4, 256
ROPE_AXIS_DIM = B_QK // 2
ROPE_THETA = 10000.0
N_GATE = 4 * A_HEADS
BATCH = 4
B_CTX_ROW = BATCH

COL_AQK, COL_AV, COL_AO, COL_AZ, COL_BQ, COL_BK, COL_BV, COL_BZ, COL_CU, COL_CZ, COL_MG = (
    0, 1, 2, 3, 4, 5, 6, 7, 8, 9, 10)
N_PROJ = 13 * D_MODEL
GATE_COL0 = 2 * A_HEADS * A_QK + A_HEADS * A_V

LANES = 128
SUBLANES = 8
VMEM_LIMIT_BYTES = 56 * 1024 * 1024

TOK_TILE = 544
MM_TM = 2176
MM_TN = 1024
ATT_TQ = 256
ATT_ROWS = 2 * ATT_TQ
QK_TILE = 256
LAST_TILE = 512
MLSTM_CHUNK = 256
FFT_COLS = 8
FFT_K1 = 16


def _cparams(*sem):
    return pltpu.CompilerParams(dimension_semantics=sem, vmem_limit_bytes=VMEM_LIMIT_BYTES)


def _split3(x):
    hi = x.astype(BF16)
    r1 = x - hi.astype(F32)
    mid = r1.astype(BF16)
    lo = (r1 - mid.astype(F32)).astype(BF16)
    return hi, mid, lo


def _dot(a, b):
    return jnp.dot(a, b, preferred_element_type=F32)


def _dot_f32_rhs(a_exact, b):
    hi, mid, lo = _split3(b)
    return _dot(a_exact, hi) + _dot(a_exact, mid) + _dot(a_exact, lo)


def _dot_f32_lhs(a, b_exact):
    hi, mid, lo = _split3(a)
    return _dot(hi, b_exact) + _dot(mid, b_exact) + _dot(lo, b_exact)


def _dot_f32(a, b):
    a_hi = a.astype(BF16)
    a_lo = (a - a_hi.astype(F32)).astype(BF16)
    b_hi = b.astype(BF16)
    b_lo = (b - b_hi.astype(F32)).astype(BF16)
    return _dot(a_hi, b_hi) + _dot(a_lo, b_hi) + _dot(a_hi, b_lo)


def _sigmoid(x):
    return 1.0 / (1.0 + jnp.exp(-x))


def _silu(x):
    return x * _sigmoid(x)


def _log_sigmoid(x):
    return jnp.minimum(x, 0.0) - jnp.log(1.0 + jnp.exp(-jnp.abs(x)))


def _mod_kernel(c_ref, w_ref, b_ref, q1_ref, k1_ref, q2_ref, k2_ref, mod_ref, lam_ref):
    sc = _silu(c_ref[...])
    mod_ref[0] = _dot_f32(sc, w_ref[0]) + b_ref[0]
    layer = jnp.full((1, LANES), pl.program_id(0), dtype=jnp.int32).astype(F32)
    lam_init = 0.8 - 0.6 * jnp.exp(-0.3 * layer)
    s1 = jnp.sum(q1_ref[0] * k1_ref[0], axis=-1, keepdims=True)
    s2 = jnp.sum(q2_ref[0] * k2_ref[0], axis=-1, keepdims=True)
    lam_ref[0] = jnp.exp(s1) - jnp.exp(s2) + lam_init


def _modulation(c_rows, w_ada, b_ada, lq1, lk1, lq2, lk2):
    n_col = 3
    vec = lambda a: a.reshape(DEPTH, 1, B_QK)
    lam_spec = pl.BlockSpec((1, 1, B_QK), lambda l, j: (l, 0, 0))
    return pl.pallas_call(
        _mod_kernel,
        grid=(DEPTH, n_col),
        in_specs=[
            pl.BlockSpec((SUBLANES, D_MODEL), lambda l, j: (0, 0)),
            pl.BlockSpec((1, D_MODEL, D_MODEL), lambda l, j: (l, 0, j)),
            pl.BlockSpec((1, 1, D_MODEL), lambda l, j: (l, 0, j)),
            lam_spec, lam_spec, lam_spec, lam_spec,
        ],
        out_specs=[
            pl.BlockSpec((1, SUBLANES, D_MODEL), lambda l, j: (l, 0, j)),
            pl.BlockSpec((1, 1, LANES), lambda l, j: (l, 0, 0)),
        ],
        out_shape=[
            jax.ShapeDtypeStruct((DEPTH, SUBLANES, 3 * D_MODEL), F32),
            jax.ShapeDtypeStruct((DEPTH, 1, LANES), F32),
        ],
        compiler_params=_cparams("arbitrary", "arbitrary"),
        name="adaln_modulation",
    )(c_rows, w_ada, b_ada.reshape(DEPTH, 1, 3 * D_MODEL), vec(lq1), vec(lk1), vec(lq2), vec(lk2))


def _ctx_rows(tile_idx, tile):
    row = tile_idx * tile + lax.broadcasted_iota(jnp.int32, (tile, 1), 0)
    return row >= N_LAT


def _norm_mod_kernel(x_ref, nw_ref, mod_ref, wg_ref, bif_ref, h_ref, g_ref):
    b = pl.program_id(0)
    x = x_ref[0]
    y = x * lax.rsqrt(jnp.mean(x * x, axis=-1, keepdims=True) + EPS) * nw_ref[...]
    is_ctx = _ctx_rows(pl.program_id(1), TOK_TILE)
    m_lat = mod_ref[pl.ds(b, 1), :]
    m_ctx = mod_ref[pl.ds(B_CTX_ROW, 1), :]
    shift = jnp.where(is_ctx, m_ctx[:, 0:D_MODEL], m_lat[:, 0:D_MODEL])
    scale = jnp.where(is_ctx, m_ctx[:, D_MODEL:2 * D_MODEL], m_lat[:, D_MODEL:2 * D_MODEL])
    h = y * (1.0 + scale) + shift
    h_ref[0] = h.astype(BF16)
    g_ref[0] = _dot_f32(h, wg_ref[...]) + bif_ref[...]


def _norm_mod(xs, norm_w, mod_l, w_gate, b_if):
    bsz = xs.shape[0]
    tok = pl.BlockSpec((1, TOK_TILE, D_MODEL), lambda b, t: (b, t, 0))
    full = lambda shape: pl.BlockSpec(shape, lambda b, t: (0,) * len(shape))
    return pl.pallas_call(
        _norm_mod_kernel,
        grid=(bsz, N_TOK // TOK_TILE),
        in_specs=[tok, full((1, D_MODEL)), full((SUBLANES, 3 * D_MODEL)),
                  full((D_MODEL, LANES)), full((1, LANES))],
        out_specs=[tok, pl.BlockSpec((1, TOK_TILE, LANES), lambda b, t: (b, t, 0))],
        out_shape=[jax.ShapeDtypeStruct((bsz, N_TOK, D_MODEL), BF16),
                   jax.ShapeDtypeStruct((bsz, N_TOK, LANES), F32)],
        compiler_params=_cparams("parallel", "parallel"),
        name="norm_modulate",
    )(xs, norm_w, mod_l, w_gate, b_if)


def _in_proj_kernel(a_ref, w0_ref, w1_ref, tail_ref, o_ref, wb_ref):
    j = pl.program_id(0)
    n_plain = GATE_COL0 // MM_TN

    @pl.when((pl.program_id(1) == 0) & (j < n_plain))
    def _():
        wb_ref[...] = w0_ref[0].astype(BF16)

    @pl.when((pl.program_id(1) == 0) & (j >= n_plain))
    def _():
        nxt = jnp.where(j == pl.num_programs(0) - 1, tail_ref[0], w1_ref[0, :N_GATE, :])
        wb_ref[:MM_TN - N_GATE, :] = w0_ref[0, N_GATE:, :].astype(BF16)
        wb_ref[MM_TN - N_GATE:, :] = nxt.astype(BF16)

    o_ref[...] = lax.dot_general(a_ref[...], wb_ref[...], (((1,), (1,)), ((), ())),
                                 preferred_element_type=F32).astype(o_ref.dtype)


def _in_projection(h2d, w_in_t, w_tail_t, layer):
    m, k = h2d.shape
    n_col = N_PROJ // MM_TN
    return pl.pallas_call(
        _in_proj_kernel,
        grid=(n_col, m // MM_TM),
        in_specs=[pl.BlockSpec((MM_TM, k), lambda j, i: (i, 0)),
                  pl.BlockSpec((1, MM_TN, k), lambda j, i: (layer, j, 0)),
                  pl.BlockSpec((1, MM_TN, k), lambda j, i: (layer, jnp.minimum(j + 1, n_col - 1), 0)),
                  pl.BlockSpec((1, N_GATE, k), lambda j, i: (layer, 0, 0))],
        out_specs=pl.BlockSpec((MM_TM, MM_TN), lambda j, i: (i, j)),
        out_shape=jax.ShapeDtypeStruct((m, N_PROJ), BF16),
        scratch_shapes=[pltpu.VMEM((MM_TN, k), BF16)],
        compiler_params=_cparams("arbitrary", "arbitrary"),
        name="in_projection",
    )(h2d, w_in_t, w_in_t, w_tail_t)


def _qk_prep_kernel(q_ref, k_ref, cos_ref, sin_ref, qw_ref, kw_ref, g_ref, qo_ref, kt_ref):
    cos = cos_ref[...]
    sin = sin_ref[...]
    group_sum = g_ref[...]
    lane = lax.broadcasted_iota(jnp.int32, (1, LANES), 1)
    first_half = (lane % ROPE_AXIS_DIM) < (ROPE_AXIS_DIM // 2)

    def prep(x_bf16, w):
        x = x_bf16.astype(F32)
        ms = _dot_f32_lhs(x * x, group_sum) * (1.0 / B_QK)
        y = x * lax.rsqrt(ms + EPS) * w
        half = ROPE_AXIS_DIM // 2
        partner = jnp.where(first_half, pltpu.roll(y, LANES - half, 1), pltpu.roll(y, half, 1))
        return y * cos + partner * sin

    for s in range(D_MODEL // LANES):
        cols = slice(s * LANES, (s + 1) * LANES)
        qo_ref[0, :, cols] = (prep(q_ref[0, :, cols], qw_ref[...]) * (B_QK ** -0.5 * LOG2E)).astype(BF16)
        kt_ref[0, cols, :] = prep(k_ref[0, :, cols], kw_ref[...]).T.astype(BF16)


def _qk_prep(proj, cos_t, sin_t, qw, kw, group_sum):
    bsz = proj.shape[0]
    col = lambda c: pl.BlockSpec((1, QK_TILE, D_MODEL), lambda b, t: (b, t, c))
    tab = pl.BlockSpec((QK_TILE, LANES), lambda b, t: (t, 0))
    full = lambda shape: pl.BlockSpec(shape, lambda b, t: (0,) * len(shape))
    return pl.pallas_call(
        _qk_prep_kernel,
        grid=(bsz, N_TOK // QK_TILE),
        in_specs=[col(COL_BQ), col(COL_BK), tab, tab, full((1, LANES)), full((1, LANES)),
                  full((LANES, LANES))],
        out_specs=[pl.BlockSpec((1, QK_TILE, D_MODEL), lambda b, t: (b, t, 0)),
                   pl.BlockSpec((1, D_MODEL, QK_TILE), lambda b, t: (b, 0, t))],
        out_shape=[jax.ShapeDtypeStruct((bsz, N_TOK, D_MODEL), BF16),
                   jax.ShapeDtypeStruct((bsz, D_MODEL, N_TOK), BF16)],
        compiler_params=_cparams("parallel", "parallel"),
        name="qk_norm_rope",
    )(proj, proj, cos_t, sin_t, qw, kw, group_sum)


def _diff_attn_body(lam_ref, q_ref, kt_ref, v_ref, z_ref, sw_ref, o_ref, vp_ref, lam_init):
    @pl.when(pl.program_id(2) == 0)
    def _():
        vp_ref[:, :B_V] = v_ref[0]
        vp_ref[:, B_V:] = jnp.ones((vp_ref.shape[0], B_V), BF16)

    chains = [slice(c * ATT_TQ, (c + 1) * ATT_TQ) for c in range(q_ref.shape[1] // ATT_TQ)]
    lane = lax.broadcasted_iota(jnp.int32, (ATT_TQ, B_V), 1)
    zero = jnp.zeros((ATT_TQ, B_V), BF16)

    def scores(rows, first_map):
        q = q_ref[0, rows, :]
        return _dot(jnp.where((lane < B_QK) if first_map else (lane >= B_QK), q, zero), kt_ref[0])

    def numerators(s):
        return jnp.exp2(s - jnp.max(s, axis=-1, keepdims=True)).astype(BF16)

    s0 = [scores(rows, True) for rows in chains]
    s1 = [scores(rows, False) for rows in chains]
    u0 = [_dot(numerators(s), vp_ref[...]) for s in s0]
    u1 = [_dot(numerators(s), vp_ref[...]) for s in s1]
    for rows, a, b in zip(chains, u0, u1):
        o = a[:, :B_V] / a[:, B_V:] - lam_ref[0] * (b[:, :B_V] / b[:, B_V:])
        y = o * lax.rsqrt(jnp.mean(o * o, axis=-1, keepdims=True) + EPS) * sw_ref[...]
        o_ref[0, rows, :] = (y * (1.0 - lam_init)
                             * _silu(z_ref[0, rows, :].astype(F32))).astype(o_ref.dtype)


def _diff_attn_kernel(lam_ref, q_ref, kt_ref, v_ref, z_ref, sw_ref, y_in_ref, o_ref, vp_ref,
                      *, lam_init):
    del y_in_ref
    _diff_attn_body(lam_ref, q_ref, kt_ref, v_ref, z_ref, sw_ref, o_ref, vp_ref, lam_init)


def _diff_attention(lam, qh, kt, proj, subln_w, lam_init, y_prev, ctx):
    bsz = qh.shape[0]
    heads_per_block = D_MODEL // B_V
    n_keys = N_CTX if ctx else N_TOK
    rows = N_CTX if ctx else ATT_ROWS
    q0 = N_LAT // rows if ctx else 0
    k0 = N_LAT // N_CTX if ctx else 0
    tile = lambda c0: pl.BlockSpec((1, rows, B_V), lambda b, h, i: (b, q0 + i, c0 + h))
    in_specs = [pl.BlockSpec(memory_space=pltpu.SMEM),
                tile(0),
                pl.BlockSpec((1, B_V, n_keys), lambda b, h, i: (b, h, k0)),
                pl.BlockSpec((1, n_keys, B_V), lambda b, h, i: (b, k0, COL_BV * heads_per_block + h)),
                tile(COL_BZ * heads_per_block),
                pl.BlockSpec((1, B_V), lambda b, h, i: (0, 0)),
                pl.BlockSpec(memory_space=pl.ANY)]
    return pl.pallas_call(
        functools.partial(_diff_attn_kernel, lam_init=lam_init),
        grid=(bsz, B_HEADS, (N_CTX if ctx else N_LAT) // rows),
        in_specs=in_specs,
        out_specs=tile(0),
        out_shape=jax.ShapeDtypeStruct((bsz, N_TOK, D_MODEL), BF16),
        scratch_shapes=[pltpu.VMEM((n_keys, 2 * B_V), BF16)],
        input_output_aliases={6: 0},
        compiler_params=_cparams("parallel", "parallel", "arbitrary"),
        name="diff_attention_ctx" if ctx else "diff_attention",
    )(lam, qh, kt, proj, proj, subln_w, y_prev)


def _mlstm_gates(gc_ref, gr_ref, reverse):
    chunk = MLSTM_CHUNK
    row = lax.broadcasted_iota(jnp.int32, (chunk, chunk), 0)
    col = lax.broadcasted_iota(jnp.int32, (chunk, chunk), 1)
    visible = (row <= col) if reverse else (row >= col)
    vis_ts = jnp.where(visible, 1.0, 0.0).astype(BF16)
    vis_st = jnp.where((row >= col) if reverse else (row <= col), 1.0, 0.0).astype(BF16)
    gate_c = gc_ref[0]
    gate_r = gr_ref[0]
    logf_r = _log_sigmoid(gate_r)
    return dict(visible=visible, gate_c=gate_c, gate_r=gate_r, logf_r=logf_r,
                cum_c=_dot_f32_rhs(vis_ts, _log_sigmoid(gate_c)),
                cum_r=_dot_f32_lhs(logf_r, vis_st))


def _mlstm_scores(g, h, reverse, q_ref, k_ref, m_ref):
    d = 1 if reverse else 0
    i_idx = (2 * d) * A_HEADS + h
    f_idx = (2 * d + 1) * A_HEADS + h
    b_c = g["cum_c"][:, f_idx:f_idx + 1]
    ic_r = g["gate_r"][i_idx:i_idx + 1, :]
    b_r = g["cum_r"][f_idx:f_idx + 1, :]
    m_st = m_ref[h, 0:1, 0:1]
    qh = q_ref[0, :, h * A_QK:(h + 1) * A_QK]
    ks = k_ref[0, :, h * A_QK:(h + 1) * A_QK].astype(F32) * (A_QK ** -0.5)
    dmat = jnp.where(g["visible"], b_c - b_r + ic_r, -jnp.inf)
    inter = b_c + m_st
    m_row = jnp.maximum(jnp.max(dmat, axis=1, keepdims=True), inter)
    s = lax.dot_general(qh, ks.astype(BF16), (((1,), (1,)), ((), ())), preferred_element_type=F32)
    return dict(
        qh=qh, ks=ks, m_st=m_st, b_c=b_c, m_row=m_row,
        ic_c=g["gate_c"][:, i_idx:i_idx + 1],
        b_last=jnp.sum(g["logf_r"][f_idx:f_idx + 1, :], axis=1, keepdims=True),
        a=jnp.exp(dmat - m_row) * s, w_inter=jnp.exp(inter - m_row))


def _mlstm_output(t, h, v_ref, o_ref, c_ref, n_ref):
    vh = v_ref[0, :, h * A_V:(h + 1) * A_V]
    n_st = n_ref[h, 0:1, :]
    num = _dot(t["a"].astype(BF16), vh) + t["w_inter"] * _dot(t["qh"], c_ref[h].astype(BF16))
    den = (jnp.sum(t["a"], axis=1, keepdims=True)
           + t["w_inter"] * jnp.sum(t["qh"].astype(F32) * n_st, axis=1, keepdims=True))
    o_ref[0, :, h * A_V:(h + 1) * A_V] = num / jnp.maximum(jnp.abs(den), jnp.exp(-t["m_row"]))


def _mlstm_state(t, h, v_ref, c_ref, n_ref, m_ref):
    vh = v_ref[0, :, h * A_V:(h + 1) * A_V]
    g = t["b_last"] - t["b_c"] + t["ic_c"]
    m_new = jnp.maximum(t["b_last"] + t["m_st"], jnp.max(g, axis=0, keepdims=True))
    kw = t["ks"] * jnp.exp(g - m_new)
    decay = jnp.exp(t["b_last"] + t["m_st"] - m_new)
    c_ref[h] = decay * c_ref[h] + _dot(kw.T.astype(BF16), vh)
    n_ref[h] = jnp.broadcast_to(decay * n_ref[h, 0:1, :] + jnp.sum(kw, axis=0, keepdims=True),
                                (SUBLANES, A_QK))
    m_ref[h] = jnp.broadcast_to(m_new, (SUBLANES, LANES))


def _mlstm_kernel(*refs):
    n_in, n_state = 5, 3
    ins = (refs[:n_in], refs[n_in:2 * n_in])
    outs = (refs[2 * n_in], refs[2 * n_in + 1])
    state = refs[2 * n_in + 2:]
    states = (state[:n_state], state[n_state:])

    @pl.when(pl.program_id(1) == 0)
    def _():
        for ref in state:
            ref[...] = jnp.zeros_like(ref)

    gates = [_mlstm_gates(ins[d][3], ins[d][4], reverse=bool(d)) for d in range(2)]
    chains = [(d, h) for h in range(A_HEADS) for d in range(2)]
    terms = {}
    for d, h in chains:
        q_ref, k_ref = ins[d][0], ins[d][1]
        terms[d, h] = _mlstm_scores(gates[d], h, bool(d), q_ref, k_ref, states[d][2])
    for d, h in chains:
        c_ref, n_ref, _ = states[d]
        _mlstm_output(terms[d, h], h, ins[d][2], outs[d], c_ref, n_ref)
    for d, h in chains:
        c_ref, n_ref, m_ref = states[d]
        _mlstm_state(terms[d, h], h, ins[d][2], c_ref, n_ref, m_ref)


def _mlstm(proj, gates, gates_t):
    bsz = proj.shape[0]
    n_lat_chunks = N_LAT // MLSTM_CHUNK

    def chunk_of(step, reverse):
        latent = (n_lat_chunks - step) if reverse else (step - 1)
        return jnp.where(step == 0, n_lat_chunks, latent)

    def specs(reverse):
        ck = functools.partial(chunk_of, reverse=reverse)
        return [
            pl.BlockSpec((1, MLSTM_CHUNK, A_HEADS * A_QK), lambda b, s: (b, ck(s), 0)),
            pl.BlockSpec((1, MLSTM_CHUNK, A_HEADS * A_QK), lambda b, s: (b, ck(s), 1)),
            pl.BlockSpec((1, MLSTM_CHUNK, D_MODEL), lambda b, s: (b, ck(s), COL_AV)),
            pl.BlockSpec((1, MLSTM_CHUNK, LANES), lambda b, s: (b, ck(s), 0)),
            pl.BlockSpec((1, N_GATE, MLSTM_CHUNK), lambda b, s: (b, 0, ck(s))),
        ]

    def out_spec(reverse):
        ck = functools.partial(chunk_of, reverse=reverse)
        return pl.BlockSpec((1, MLSTM_CHUNK, D_MODEL), lambda b, s: (b, ck(s), 0))

    state = [pltpu.VMEM((A_HEADS, A_QK, A_V), F32),
             pltpu.VMEM((A_HEADS, SUBLANES, A_QK), F32),
             pltpu.VMEM((A_HEADS, SUBLANES, LANES), F32)]
    args = (proj, proj, proj, gates, gates_t)
    return pl.pallas_call(
        _mlstm_kernel,
        grid=(bsz, N_TOK // MLSTM_CHUNK),
        in_specs=specs(False) + specs(True),
        out_specs=[out_spec(False), out_spec(True)],
        out_shape=[jax.ShapeDtypeStruct((bsz, N_TOK, D_MODEL), F32)] * 2,
        scratch_shapes=state + state,
        compiler_params=_cparams("parallel", "arbitrary"),
        name="mlstm_bidirectional",
    )(*args, *args)


def _mlstm_out_kernel(hf_ref, hb_ref, o_ref, z_ref, w_ref, y_ref):
    for h in range(A_HEADS):
        cols = slice(h * A_V, (h + 1) * A_V)
        x = hf_ref[0, :, cols] + hb_ref[0, :, cols]
        xn = x * lax.rsqrt(jnp.mean(x * x, axis=-1, keepdims=True) + EPS) * w_ref[:, cols]
        gate = _sigmoid(o_ref[0, :, cols].astype(F32)) * _silu(z_ref[0, :, cols].astype(F32))
        y_ref[0, :, cols] = (xn * gate).astype(y_ref.dtype)


def _mlstm_out(h_fwd, h_bwd, proj, a_norm_w):
    bsz = proj.shape[0]
    col = lambda c: pl.BlockSpec((1, TOK_TILE, D_MODEL), lambda b, t: (b, t, c))
    return pl.pallas_call(
        _mlstm_out_kernel,
        grid=(bsz, N_TOK // TOK_TILE),
        in_specs=[col(0), col(0), col(COL_AO), col(COL_AZ),
                  pl.BlockSpec((1, D_MODEL), lambda b, t: (0, 0))],
        out_specs=col(0),
        out_shape=jax.ShapeDtypeStruct((bsz, N_TOK, D_MODEL), BF16),
        compiler_params=_cparams("parallel", "parallel"),
        name="mlstm_out",
    )(h_fwd, h_bwd, proj, proj, a_norm_w)


def _chan_dft_kernel(u_ref, cs_ref, pq_ref):
    for g in range(C_GROUPS):
        cols = slice(g * C_GROUP_DIM, (g + 1) * C_GROUP_DIM)
        r = _dot(u_ref[0, :, cols], cs_ref[...])
        pq_ref[0, 0, :, cols] = r[:, :C_GROUP_DIM].astype(pq_ref.dtype)
        pq_ref[0, 1, :, cols] = r[:, C_GROUP_DIM:].astype(pq_ref.dtype)


def _chan_dft(proj, cs):
    bsz = proj.shape[0]
    return pl.pallas_call(
        _chan_dft_kernel,
        grid=(bsz, N_TOK // TOK_TILE),
        in_specs=[pl.BlockSpec((1, TOK_TILE, D_MODEL), lambda b, t: (b, t, COL_CU)),
                  pl.BlockSpec((C_GROUP_DIM, 2 * C_GROUP_DIM), lambda b, t: (0, 0))],
        out_specs=pl.BlockSpec((1, 2, TOK_TILE, D_MODEL), lambda b, t: (b, 0, t, 0)),
        out_shape=jax.ShapeDtypeStruct((bsz, 2, N_TOK, D_MODEL), BF16),
        compiler_params=_cparams("parallel", "parallel"),
        name="fourier_channel_dft",
    )(proj, cs)


def _row_dft_kernel(pq_ref, cs_ref, tc_ref, ts_ref, z_ref):
    cos, sin = cs_ref[0], cs_ref[1]
    reps = D_MODEL // LANES
    for j in range(FFT_COLS):
        cols = slice(j * D_MODEL, (j + 1) * D_MODEL)
        p, q = pq_ref[0, 0, :, cols], pq_ref[0, 1, :, cols]
        a_re = _dot(cos, p) - _dot(sin, q)
        a_im = _dot(cos, q) + _dot(sin, p)
        tc = jnp.tile(tc_ref[j], (1, reps))
        ts = jnp.tile(ts_ref[j], (1, reps))
        z_ref[0, 0, :, cols] = (a_re * tc - a_im * ts).astype(z_ref.dtype)
        z_ref[0, 1, :, cols] = (a_re * ts + a_im * tc).astype(z_ref.dtype)


def _col_dft_kernel(z_ref, cs_ref, g_ref, y_in_ref, y_ref):
    del y_in_ref
    rows = FFT_K1 * GRID_W
    z_re = z_ref[0, 0].reshape(rows, D_MODEL)
    z_im = z_ref[0, 1].reshape(rows, D_MODEL)
    y = _dot(cs_ref[0], z_re) - _dot(cs_ref[1], z_im)
    g = g_ref[0].reshape(rows, D_MODEL).astype(F32)
    y_ref[0] = (y * _silu(g)).astype(y_ref.dtype).reshape(GRID_W, FFT_K1, D_MODEL)


def _seq_dft_latent(pq, proj, tables, y_prev):
    bsz = proj.shape[0]
    side = GRID_W
    cs_a, cs_b, tw_c, tw_s = tables
    n_rows_all = N_TOK // side
    full = lambda shape: pl.BlockSpec(shape, lambda b, t: (0,) * len(shape))
    z = pl.pallas_call(
        _row_dft_kernel,
        grid=(bsz, side // FFT_COLS),
        in_specs=[pl.BlockSpec((1, 2, side, FFT_COLS * D_MODEL), lambda b, t: (b, 0, 0, t)),
                  full((2, side, side)),
                  pl.BlockSpec((FFT_COLS, side, LANES), lambda b, t: (t, 0, 0)),
                  pl.BlockSpec((FFT_COLS, side, LANES), lambda b, t: (t, 0, 0))],
        out_specs=pl.BlockSpec((1, 2, side, FFT_COLS * D_MODEL), lambda b, t: (b, 0, 0, t)),
        out_shape=jax.ShapeDtypeStruct((bsz, 2, side, side * D_MODEL), BF16),
        compiler_params=_cparams("parallel", "parallel"),
        name="fourier_row_dft",
    )(pq.reshape(bsz, 2, n_rows_all, side * D_MODEL), cs_a, tw_c, tw_s)
    y = pl.pallas_call(
        _col_dft_kernel,
        grid=(bsz, side // FFT_K1),
        in_specs=[pl.BlockSpec((1, 2, FFT_K1, side, D_MODEL), lambda b, t: (b, 0, t, 0, 0)),
                  full((2, FFT_K1 * side, FFT_K1 * side)),
                  pl.BlockSpec((1, side, FFT_K1, D_MODEL), lambda b, t: (b, 0, t, COL_CZ)),
                  pl.BlockSpec(memory_space=pl.ANY)],
        out_specs=pl.BlockSpec((1, side, FFT_K1, D_MODEL), lambda b, t: (b, 0, t, 0)),
        out_shape=jax.ShapeDtypeStruct((bsz, n_rows_all, side, D_MODEL), BF16),
        input_output_aliases={3: 0},
        compiler_params=_cparams("parallel", "parallel"),
        name="fourier_col_dft",
    )(z.reshape(bsz, 2, side, side, D_MODEL), cs_b, proj.reshape(bsz, n_rows_all, side, N_PROJ),
      y_prev.reshape(bsz, n_rows_all, side, D_MODEL))
    return y.reshape(bsz, N_TOK, D_MODEL)


def _seq_dft_ctx_kernel(a_ref, pq_ref, z_ref, y_in_ref, y_ref):
    del y_in_ref
    r = _dot(a_ref[:, :N_CTX], pq_ref[0, 0]) + _dot(a_ref[:, N_CTX:], pq_ref[0, 1])
    y_ref[0] = (r * _silu(z_ref[0].astype(F32))).astype(y_ref.dtype)


def _seq_dft_ctx(dft_ctx, pq, proj, y_c):
    bsz = proj.shape[0]
    ctx_blk = N_LAT // N_CTX
    return pl.pallas_call(
        _seq_dft_ctx_kernel,
        grid=(bsz,),
        in_specs=[
            pl.BlockSpec((N_CTX, 2 * N_CTX), lambda b: (0, 0)),
            pl.BlockSpec((1, 2, N_CTX, D_MODEL), lambda b: (b, 0, ctx_blk, 0)),
            pl.BlockSpec((1, N_CTX, D_MODEL), lambda b: (b, ctx_blk, COL_CZ)),
            pl.BlockSpec(memory_space=pl.ANY),
        ],
        out_specs=pl.BlockSpec((1, N_CTX, D_MODEL), lambda b: (b, ctx_blk, 0)),
        out_shape=jax.ShapeDtypeStruct(y_c.shape, y_c.dtype),
        input_output_aliases={3: 0},
        compiler_params=_cparams("parallel"),
        name="fourier_token_dft_ctx",
    )(dft_ctx, pq, proj, y_c)


def _merge_kernel(x_ref, ya_ref, yb_ref, yc_ref, ga_ref, gb_ref, gc_ref, mod_ref,
                  wa_ref, wb_ref, wc_ref, wo_ref, o_ref, *, tile):
    b = pl.program_id(0)

    def branch(y_ref, g_ref, w_ref):
        return _sigmoid(g_ref[0].astype(F32)) * _dot(y_ref[0], w_ref[...])

    y = branch(ya_ref, ga_ref, wa_ref) + branch(yb_ref, gb_ref, wb_ref) + branch(yc_ref, gc_ref, wc_ref)
    out = _dot(y.astype(BF16), wo_ref[...])
    gate_cols = slice(2 * D_MODEL, 3 * D_MODEL)
    gate = jnp.where(_ctx_rows(pl.program_id(1), tile),
                     mod_ref[pl.ds(B_CTX_ROW, 1), gate_cols], mod_ref[pl.ds(b, 1), gate_cols])
    o_ref[0] = x_ref[0] + gate * out


def _merge(xs, y_a, y_b, y_c, proj, mod_l, wa, wb, wc, wo, n_rows, tile):
    bsz = xs.shape[0]
    col = lambda c: pl.BlockSpec((1, tile, D_MODEL), lambda b, t: (b, t, c))
    weight = pl.BlockSpec((D_MODEL, D_MODEL), lambda b, t: (0, 0))
    return pl.pallas_call(
        functools.partial(_merge_kernel, tile=tile),
        grid=(bsz, n_rows // tile),
        in_specs=[col(0), col(0), col(0), col(0), col(COL_MG), col(COL_MG + 1), col(COL_MG + 2),
                  pl.BlockSpec((SUBLANES, 3 * D_MODEL), lambda b, t: (0, 0)),
                  weight, weight, weight, weight],
        out_specs=col(0),
        out_shape=jax.ShapeDtypeStruct((bsz, n_rows, D_MODEL), F32),
        compiler_params=_cparams("parallel", "parallel"),
        name="merge_out_residual",
    )(xs, y_a, y_b, y_c, proj, proj, proj, mod_l, wa, wb, wc, wo)


def _rope_tables():
    n = jnp.arange(N_LAT, dtype=jnp.int32)
    row = (n // GRID_W).astype(F32)
    col = (n % GRID_W).astype(F32)
    inv_freq = ROPE_THETA ** (-jnp.arange(0, ROPE_AXIS_DIM, 2, dtype=F32) / ROPE_AXIS_DIM)
    ang_r = row[:, None] * inv_freq
    ang_c = col[:, None] * inv_freq
    cos = jnp.concatenate([jnp.cos(ang_r)] * 2 + [jnp.cos(ang_c)] * 2, axis=-1)
    sin = jnp.concatenate([-jnp.sin(ang_r), jnp.sin(ang_r), -jnp.sin(ang_c), jnp.sin(ang_c)], axis=-1)
    cos = jnp.concatenate([cos, jnp.ones((N_CTX, B_QK), F32)], axis=0)
    sin = jnp.concatenate([sin, jnp.zeros((N_CTX, B_QK), F32)], axis=0)
    reps = LANES // B_QK
    return jnp.tile(cos, (1, reps)), jnp.tile(sin, (1, reps))


def _dft_cos_sin(n, scale):
    idx = jnp.arange(n, dtype=jnp.int32)
    ang = ((idx[:, None] * idx[None, :]) % n).astype(F32) * (2.0 * math.pi / n)
    return jnp.cos(ang) * scale, jnp.sin(ang) * scale


def _dft_tables():
    c_ch, s_ch = _dft_cos_sin(C_GROUP_DIM, 1.0)
    chan = jnp.concatenate([c_ch, s_ch], axis=1).astype(BF16)
    c_c, s_c = _dft_cos_sin(N_CTX, (N_CTX * C_GROUP_DIM) ** -0.5)
    ctx = jnp.concatenate([c_c, -s_c], axis=1).astype(BF16)
    c_s, s_s = _dft_cos_sin(GRID_W, 1.0)
    stage_a = jnp.stack([c_s, s_s]).astype(BF16)
    eye = jnp.eye(FFT_K1, dtype=F32)
    expand = lambda t: jnp.einsum("kn,ij->kijn", t, eye).reshape(
        FFT_K1 * GRID_W, FFT_K1 * GRID_W)
    stage_b = (jnp.stack([expand(c_s), expand(s_s)]) * (N_LAT * C_GROUP_DIM) ** -0.5).astype(BF16)
    idx = jnp.arange(GRID_W, dtype=jnp.int32)
    ang = (idx[:, None] * idx[None, :]).astype(F32) * (2.0 * math.pi / N_LAT)
    lanes = lambda t: jnp.broadcast_to(t[:, :, None], (GRID_W, GRID_W, LANES))
    return chan, ctx, (stage_a, stage_b, lanes(jnp.cos(ang)), lanes(jnp.sin(ang)))


def _group_sum_matrix():
    g = np.arange(LANES) // B_QK
    return jnp.asarray(g[:, None] == g[None, :], dtype=BF16)


def kernel(x, c, ctx, c_ctx, norm_w, w_ada, b_ada, w_in, b_if, a_norm_w, q_norm_w, k_norm_w,
           lambda_q1, lambda_k1, lambda_q2, lambda_k2, subln_w, w_a_out, w_b_out, w_c_out, w_out):
    bsz = x.shape[0]
    xs = jnp.concatenate([x, ctx], axis=1)
    c_rows = jnp.concatenate(
        [c, c_ctx[None, :], jnp.zeros((SUBLANES - bsz - 1, D_MODEL), F32)], axis=0)
    mod, lam = _modulation(c_rows, w_ada, b_ada, lambda_q1, lambda_k1, lambda_q2, lambda_k2)

    cos_t, sin_t = _rope_tables()
    dft_chan, dft_ctx, dft_lat = _dft_tables()
    group_sum = _group_sum_matrix()

    w_in_t = jnp.swapaxes(w_in, 1, 2)
    w_tail_t = w_in_t[:, w_in_t.shape[1] - N_GATE:, :]
    lane_pad = LANES - N_GATE
    pad_lanes = lambda w: jnp.pad(w, ((0, 0),) * (w.ndim - 1) + ((0, lane_pad),))
    w_gate = pad_lanes(w_in[:, :, GATE_COL0:GATE_COL0 + N_GATE])
    b_gate = pad_lanes(b_if)
    wa, wb, wc, wo = (w.astype(BF16) for w in (w_a_out, w_b_out, w_c_out, w_out))
    tile2 = lambda w: jnp.tile(w[None, :], (1, LANES // B_QK))

    for l in range(DEPTH):
        last = l == DEPTH - 1
        h, gates = _norm_mod(xs, norm_w[l][None, :], mod[l], w_gate[l], b_gate[l][None, :])
        proj = _in_projection(h.reshape(bsz * N_TOK, D_MODEL), w_in_t, w_tail_t, l).reshape(bsz, N_TOK, N_PROJ)

        gates_t = jnp.swapaxes(gates[:, :, :N_GATE], 1, 2)
        h_fwd, h_bwd = _mlstm(proj, gates, gates_t)
        y_a = _mlstm_out(h_fwd, h_bwd, proj, a_norm_w[l][None, :])

        qh, kt = _qk_prep(proj, cos_t, sin_t, tile2(q_norm_w[l]), tile2(k_norm_w[l]), group_sum)
        lam_l = lam[l, 0, :1]
        lam_init = 0.8 - 0.6 * math.exp(-0.3 * l)
        y_b = _diff_attention(lam_l, qh, kt, proj, subln_w[l][None, :], lam_init, h, ctx=False)
        if not last:
            y_b = _diff_attention(lam_l, qh, kt, proj, subln_w[l][None, :], lam_init, y_b, ctx=True)

        pq = _chan_dft(proj, dft_chan)
        y_c = _seq_dft_latent(pq, proj, dft_lat, qh)

        if last:
            return _merge(xs, y_a, y_b, y_c, proj, mod[l], wa[l], wb[l], wc[l], wo[l],
                          n_rows=N_LAT, tile=LAST_TILE)
        y_c = _seq_dft_ctx(dft_ctx, pq, proj, y_c)
        xs = _merge(xs, y_a, y_b, y_c, proj, mod[l], wa[l], wb[l], wc[l], wo[l],
                    n_rows=N_TOK, tile=TOK_TILE)
```

```python
import functools
import math

import jax
import jax.numpy as jnp
import numpy as np
from jax import lax
from jax.experimental import pallas as pl
from jax.experimental.pallas import tpu as pltpu

F32 = jnp.float32
BF16 = jnp.bfloat16
LOG2E = math.log2(math.e)

D_MODEL = 1024
DEPTH = 4
N_LAT = 4096
N_CTX = 256
N_TOK = N_LAT + N_CTX
GRID_W = 64
EPS = 1e-6

A_HEADS, A_QK, A_V = 4, 128, 256
B_HEADS, B_QK, B_V = 8, 64, 128
C_GROUPS, C_GROUP_DIM = 4, 256
ROPE_AXIS_DIM = B_QK // 2
ROPE_THETA = 10000.0
N_GATE = 4 * A_HEADS
BATCH = 4
B_CTX_ROW = BATCH

COL_AQK, COL_AV, COL_AO, COL_AZ, COL_BQ, COL_BK, COL_BV, COL_BZ, COL_CU, COL_CZ, COL_MG = (
    0, 1, 2, 3, 4, 5, 6, 7, 8, 9, 10)
N_PROJ = 13 * D_MODEL
GATE_COL0 = 2 * A_HEADS * A_QK + A_HEADS * A_V

LANES = 128
SUBLANES = 8
VMEM_LIMIT_BYTES = 56 * 1024 * 1024

TOK_TILE = 544
MM_TM = 2176
MM_TN = 1024
ATT_TQ = 256
ATT_ROWS = 2 * ATT_TQ
QK_TILE = 256
LAST_TILE = 512
MLSTM_CHUNK = 256
CH_ROWS = 16
CH_PITCH = GRID_W + 8
FFT_COLS = 8
FFT_K1 = 16


def _cparams(*sem):
    return pltpu.CompilerParams(dimension_semantics=sem, vmem_limit_bytes=VMEM_LIMIT_BYTES)


def _split3(x):
    hi = x.astype(BF16)
    r1 = x - hi.astype(F32)
    mid = r1.astype(BF16)
    lo = (r1 - mid.astype(F32)).astype(BF16)
    return hi, mid, lo


def _dot(a, b):
    return jnp.dot(a, b, preferred_element_type=F32)


def _dot_f32_rhs(a_exact, b):
    hi, mid, lo = _split3(b)
    return _dot(a_exact, hi) + _dot(a_exact, mid) + _dot(a_exact, lo)


def _dot_f32_lhs(a, b_exact):
    hi, mid, lo = _split3(a)
    return _dot(hi, b_exact) + _dot(mid, b_exact) + _dot(lo, b_exact)


def _dot_f32(a, b):
    a_hi = a.astype(BF16)
    a_lo = (a - a_hi.astype(F32)).astype(BF16)
    b_hi = b.astype(BF16)
    b_lo = (b - b_hi.astype(F32)).astype(BF16)
    return _dot(a_hi, b_hi) + _dot(a_lo, b_hi) + _dot(a_hi, b_lo)


def _sigmoid(x):
    return 1.0 / (1.0 + jnp.exp(-x))


def _silu(x):
    return x * _sigmoid(x)


def _log_sigmoid(x):
    return jnp.minimum(x, 0.0) - jnp.log(1.0 + jnp.exp(-jnp.abs(x)))


def _mod_kernel(c_ref, w_ref, b_ref, q1_ref, k1_ref, q2_ref, k2_ref, mod_ref, lam_ref):
    sc = _silu(c_ref[...])
    mod_ref[0] = _dot_f32(sc, w_ref[0]) + b_ref[0]
    layer = jnp.full((1, LANES), pl.program_id(0), dtype=jnp.int32).astype(F32)
    lam_init = 0.8 - 0.6 * jnp.exp(-0.3 * layer)
    s1 = jnp.sum(q1_ref[0] * k1_ref[0], axis=-1, keepdims=True)
    s2 = jnp.sum(q2_ref[0] * k2_ref[0], axis=-1, keepdims=True)
    lam_ref[0] = jnp.exp(s1) - jnp.exp(s2) + lam_init


def _modulation(c_rows, w_ada, b_ada, lq1, lk1, lq2, lk2):
    n_col = 3
    vec = lambda a: a.reshape(DEPTH, 1, B_QK)
    lam_spec = pl.BlockSpec((1, 1, B_QK), lambda l, j: (l, 0, 0))
    return pl.pallas_call(
        _mod_kernel,
        grid=(DEPTH, n_col),
        in_specs=[
            pl.BlockSpec((SUBLANES, D_MODEL), lambda l, j: (0, 0)),
            pl.BlockSpec((1, D_MODEL, D_MODEL), lambda l, j: (l, 0, j)),
            pl.BlockSpec((1, 1, D_MODEL), lambda l, j: (l, 0, j)),
            lam_spec, lam_spec, lam_spec, lam_spec,
        ],
        out_specs=[
            pl.BlockSpec((1, SUBLANES, D_MODEL), lambda l, j: (l, 0, j)),
            pl.BlockSpec((1, 1, LANES), lambda l, j: (l, 0, 0)),
        ],
        out_shape=[
            jax.ShapeDtypeStruct((DEPTH, SUBLANES, 3 * D_MODEL), F32),
            jax.ShapeDtypeStruct((DEPTH, 1, LANES), F32),
        ],
        compiler_params=_cparams("arbitrary", "arbitrary"),
        name="adaln_modulation",
    )(c_rows, w_ada, b_ada.reshape(DEPTH, 1, 3 * D_MODEL), vec(lq1), vec(lk1), vec(lq2), vec(lk2))


def _ctx_rows(tile_idx, tile):
    row = tile_idx * tile + lax.broadcasted_iota(jnp.int32, (tile, 1), 0)
    return row >= N_LAT


def _norm_mod_kernel(x_ref, nw_ref, mod_ref, wg_ref, bif_ref, h_ref, g_ref):
    b = pl.program_id(0)
    x = x_ref[0]
    y = x * lax.rsqrt(jnp.mean(x * x, axis=-1, keepdims=True) + EPS) * nw_ref[...]
    is_ctx = _ctx_rows(pl.program_id(1), TOK_TILE)
    m_lat = mod_ref[pl.ds(b, 1), :]
    m_ctx = mod_ref[pl.ds(B_CTX_ROW, 1), :]
    shift = jnp.where(is_ctx, m_ctx[:, 0:D_MODEL], m_lat[:, 0:D_MODEL])
    scale = jnp.where(is_ctx, m_ctx[:, D_MODEL:2 * D_MODEL], m_lat[:, D_MODEL:2 * D_MODEL])
    h = y * (1.0 + scale) + shift
    h_ref[0] = h.astype(BF16)
    g_ref[0] = _dot_f32(h, wg_ref[...]) + bif_ref[...]


def _norm_mod(xs, norm_w, mod_l, w_gate, b_if):
    bsz = xs.shape[0]
    tok = pl.BlockSpec((1, TOK_TILE, D_MODEL), lambda b, t: (b, t, 0))
    full = lambda shape: pl.BlockSpec(shape, lambda b, t: (0,) * len(shape))
    return pl.pallas_call(
        _norm_mod_kernel,
        grid=(bsz, N_TOK // TOK_TILE),
        in_specs=[tok, full((1, D_MODEL)), full((SUBLANES, 3 * D_MODEL)),
                  full((D_MODEL, LANES)), full((1, LANES))],
        out_specs=[tok, pl.BlockSpec((1, TOK_TILE, LANES), lambda b, t: (b, t, 0))],
        out_shape=[jax.ShapeDtypeStruct((bsz, N_TOK, D_MODEL), BF16),
                   jax.ShapeDtypeStruct((bsz, N_TOK, LANES), F32)],
        compiler_params=_cparams("parallel", "parallel"),
        name="norm_modulate",
    )(xs, norm_w, mod_l, w_gate, b_if)


def _in_proj_kernel(a_ref, w0_ref, w1_ref, tail_ref, o_ref, wb_ref):
    j = pl.program_id(0)
    n_plain = GATE_COL0 // MM_TN

    @pl.when((pl.program_id(1) == 0) & (j < n_plain))
    def _():
        wb_ref[...] = w0_ref[0].astype(BF16)

    @pl.when((pl.program_id(1) == 0) & (j >= n_plain))
    def _():
        nxt = jnp.where(j == pl.num_programs(0) - 1, tail_ref[0], w1_ref[0, :N_GATE, :])
        wb_ref[:MM_TN - N_GATE, :] = w0_ref[0, N_GATE:, :].astype(BF16)
        wb_ref[MM_TN - N_GATE:, :] = nxt.astype(BF16)

    o_ref[...] = lax.dot_general(a_ref[...], wb_ref[...], (((1,), (1,)), ((), ())),
                                 preferred_element_type=F32).astype(o_ref.dtype)


def _in_projection(h2d, w_in_t, w_tail_t, layer):
    m, k = h2d.shape
    n_col = N_PROJ // MM_TN
    return pl.pallas_call(
        _in_proj_kernel,
        grid=(n_col, m // MM_TM),
        in_specs=[pl.BlockSpec((MM_TM, k), lambda j, i: (i, 0)),
                  pl.BlockSpec((1, MM_TN, k), lambda j, i: (layer, j, 0)),
                  pl.BlockSpec((1, MM_TN, k), lambda j, i: (layer, jnp.minimum(j + 1, n_col - 1), 0)),
                  pl.BlockSpec((1, N_GATE, k), lambda j, i: (layer, 0, 0))],
        out_specs=pl.BlockSpec((MM_TM, MM_TN), lambda j, i: (i, j)),
        out_shape=jax.ShapeDtypeStruct((m, N_PROJ), BF16),
        scratch_shapes=[pltpu.VMEM((MM_TN, k), BF16)],
        compiler_params=_cparams("arbitrary", "arbitrary"),
        name="in_projection",
    )(h2d, w_in_t, w_in_t, w_tail_t)


def _qk_prep_kernel(q_ref, k_ref, cos_ref, sin_ref, qw_ref, kw_ref, g_ref, qo_ref, kt_ref):
    cos = cos_ref[...]
    sin = sin_ref[...]
    group_sum = g_ref[...]
    lane = lax.broadcasted_iota(jnp.int32, (1, LANES), 1)
    first_half = (lane % ROPE_AXIS_DIM) < (ROPE_AXIS_DIM // 2)

    def prep(x_bf16, w):
        x = x_bf16.astype(F32)
        ms = _dot_f32_lhs(x * x, group_sum) * (1.0 / B_QK)
        y = x * lax.rsqrt(ms + EPS) * w
        half = ROPE_AXIS_DIM // 2
        partner = jnp.where(first_half, pltpu.roll(y, LANES - half, 1), pltpu.roll(y, half, 1))
        return y * cos + partner * sin

    for s in range(D_MODEL // LANES):
        cols = slice(s * LANES, (s + 1) * LANES)
        qo_ref[0, :, cols] = (prep(q_ref[0, :, cols], qw_ref[...]) * (B_QK ** -0.5 * LOG2E)).astype(BF16)
        kt_ref[0, cols, :] = prep(k_ref[0, :, cols], kw_ref[...]).T.astype(BF16)


def _qk_prep(proj, cos_t, sin_t, qw, kw, group_sum):
    bsz = proj.shape[0]
    col = lambda c: pl.BlockSpec((1, QK_TILE, D_MODEL), lambda b, t: (b, t, c))
    tab = pl.BlockSpec((QK_TILE, LANES), lambda b, t: (t, 0))
    full = lambda shape: pl.BlockSpec(shape, lambda b, t: (0,) * len(shape))
    return pl.pallas_call(
        _qk_prep_kernel,
        grid=(bsz, N_TOK // QK_TILE),
        in_specs=[col(COL_BQ), col(COL_BK), tab, tab, full((1, LANES)), full((1, LANES)),
                  full((LANES, LANES))],
        out_specs=[pl.BlockSpec((1, QK_TILE, D_MODEL), lambda b, t: (b, t, 0)),
                   pl.BlockSpec((1, D_MODEL, QK_TILE), lambda b, t: (b, 0, t))],
        out_shape=[jax.ShapeDtypeStruct((bsz, N_TOK, D_MODEL), BF16),
                   jax.ShapeDtypeStruct((bsz, D_MODEL, N_TOK), BF16)],
        compiler_params=_cparams("parallel", "parallel"),
        name="qk_norm_rope",
    )(proj, proj, cos_t, sin_t, qw, kw, group_sum)


def _diff_attn_body(lam_ref, q_ref, kt_ref, v_ref, z_ref, sw_ref, o_ref, vp_ref, lam_init):
    @pl.when(pl.program_id(2) == 0)
    def _():
        vp_ref[:, :B_V] = v_ref[0]
        vp_ref[:, B_V:] = jnp.ones((vp_ref.shape[0], B_V), BF16)

    chains = [slice(c * ATT_TQ, (c + 1) * ATT_TQ) for c in range(q_ref.shape[1] // ATT_TQ)]
    lane = lax.broadcasted_iota(jnp.int32, (ATT_TQ, B_V), 1)
    zero = jnp.zeros((ATT_TQ, B_V), BF16)

    def scores(rows, first_map):
        q = q_ref[0, rows, :]
        return _dot(jnp.where((lane < B_QK) if first_map else (lane >= B_QK), q, zero), kt_ref[0])

    def numerators(s):
        return jnp.exp2(s - jnp.max(s, axis=-1, keepdims=True)).astype(BF16)

    s0 = [scores(rows, True) for rows in chains]
    s1 = [scores(rows, False) for rows in chains]
    u0 = [_dot(numerators(s), vp_ref[...]) for s in s0]
    u1 = [_dot(numerators(s), vp_ref[...]) for s in s1]
    for rows, a, b in zip(chains, u0, u1):
        o = a[:, :B_V] / a[:, B_V:] - lam_ref[0] * (b[:, :B_V] / b[:, B_V:])
        y = o * lax.rsqrt(jnp.mean(o * o, axis=-1, keepdims=True) + EPS) * sw_ref[...]
        o_ref[0, rows, :] = (y * (1.0 - lam_init)
                             * _silu(z_ref[0, rows, :].astype(F32))).astype(o_ref.dtype)


def _diff_attn_kernel(lam_ref, q_ref, kt_ref, v_ref, z_ref, sw_ref, y_in_ref, o_ref, vp_ref,
                      *, lam_init):
    del y_in_ref
    _diff_attn_body(lam_ref, q_ref, kt_ref, v_ref, z_ref, sw_ref, o_ref, vp_ref, lam_init)


def _diff_attention(lam, qh, kt, proj, subln_w, lam_init, y_prev, ctx):
    bsz = qh.shape[0]
    heads_per_block = D_MODEL // B_V
    n_keys = N_CTX if ctx else N_TOK
    rows = N_CTX if ctx else ATT_ROWS
    q0 = N_LAT // rows if ctx else 0
    k0 = N_LAT // N_CTX if ctx else 0
    tile = lambda c0: pl.BlockSpec((1, rows, B_V), lambda b, h, i: (b, q0 + i, c0 + h))
    in_specs = [pl.BlockSpec(memory_space=pltpu.SMEM),
                tile(0),
                pl.BlockSpec((1, B_V, n_keys), lambda b, h, i: (b, h, k0)),
                pl.BlockSpec((1, n_keys, B_V), lambda b, h, i: (b, k0, COL_BV * heads_per_block + h)),
                tile(COL_BZ * heads_per_block),
                pl.BlockSpec((1, B_V), lambda b, h, i: (0, 0)),
                pl.BlockSpec(memory_space=pl.ANY)]
    return pl.pallas_call(
        functools.partial(_diff_attn_kernel, lam_init=lam_init),
        grid=(bsz, B_HEADS, (N_CTX if ctx else N_LAT) // rows),
        in_specs=in_specs,
        out_specs=tile(0),
        out_shape=jax.ShapeDtypeStruct((bsz, N_TOK, D_MODEL), BF16),
        scratch_shapes=[pltpu.VMEM((n_keys, 2 * B_V), BF16)],
        input_output_aliases={6: 0},
        compiler_params=_cparams("parallel", "parallel", "arbitrary"),
        name="diff_attention_ctx" if ctx else "diff_attention",
    )(lam, qh, kt, proj, proj, subln_w, y_prev)


def _mlstm_gates(gc_ref, gr_ref, reverse):
    chunk = MLSTM_CHUNK
    row = lax.broadcasted_iota(jnp.int32, (chunk, chunk), 0)
    col = lax.broadcasted_iota(jnp.int32, (chunk, chunk), 1)
    seen = (row >= col) if reverse else (row <= col)
    vis_ts = jnp.where((row <= col) if reverse else (row >= col), 1.0, 0.0).astype(BF16)
    vis_st = jnp.where(seen, 1.0, 0.0).astype(BF16)
    gate_c = gc_ref[0]
    gate_r = gr_ref[0]
    logf_r = _log_sigmoid(gate_r)
    return dict(seen=seen, gate_c=gate_c, gate_r=gate_r, logf_r=logf_r,
                cum_c=_dot_f32_rhs(vis_ts, _log_sigmoid(gate_c)),
                cum_r=_dot_f32_lhs(logf_r, vis_st))


def _mlstm_scores(g, h, reverse, q_ref, k_ref, m_ref):
    d = 1 if reverse else 0
    i_idx = (2 * d) * A_HEADS + h
    f_idx = (2 * d + 1) * A_HEADS + h
    src = g["gate_c"][:, i_idx:i_idx + 1] - g["cum_c"][:, f_idx:f_idx + 1]
    ic_r = g["gate_r"][i_idx:i_idx + 1, :]
    b_r = g["cum_r"][f_idx:f_idx + 1, :]
    m_st = m_ref[h, 0:1, 0:1]
    q_t = q_ref[0, :, h * A_QK:(h + 1) * A_QK].astype(F32).T.astype(BF16)
    ks = (k_ref[0, :, h * A_QK:(h + 1) * A_QK].astype(F32) * (A_QK ** -0.5)).astype(BF16)
    dmat_t = jnp.where(g["seen"], src + b_r, -jnp.inf)
    inter = b_r + m_st
    m_row = jnp.maximum(jnp.max(dmat_t, axis=0, keepdims=True), inter)
    return dict(
        q_t=q_t, ks=ks, m_st=m_st, b_r=b_r, ic_r=ic_r, m_row=m_row,
        b_last=jnp.sum(g["logf_r"][f_idx:f_idx + 1, :], axis=1, keepdims=True),
        a_t=jnp.exp(dmat_t - m_row) * _dot(ks, q_t), w_inter=jnp.exp(inter - m_row))


def _mlstm_output(t, h, v_ref, o_ref, ct_ref, n_ref):
    v_t = v_ref[0, :, h * A_V:(h + 1) * A_V].astype(F32).T.astype(BF16)
    t["v_t"] = v_t
    num_t = (_dot(v_t, t["a_t"].astype(BF16))
             + t["w_inter"] * _dot(ct_ref[h].astype(BF16), t["q_t"]))
    qn = _dot(n_ref[h].astype(BF16), t["q_t"])[0:1, :]
    den = jnp.sum(t["a_t"], axis=0, keepdims=True) + t["w_inter"] * qn
    h_t = num_t / jnp.maximum(jnp.abs(den), jnp.exp(-t["m_row"]))
    o_ref[0, :, h * A_V:(h + 1) * A_V] = h_t.T


def _mlstm_state(t, h, ct_ref, n_ref, m_ref):
    g = t["b_last"] - t["b_r"] + t["ic_r"]
    m_new = jnp.maximum(t["b_last"] + t["m_st"], jnp.max(g, axis=1, keepdims=True))
    wk = jnp.exp(g - m_new)
    decay = jnp.exp(t["b_last"] + t["m_st"] - m_new)
    vw_t = (t["v_t"].astype(F32) * wk).astype(BF16)
    ct_ref[h] = decay * ct_ref[h] + _dot(vw_t, t["ks"])
    wk_rows = jnp.broadcast_to(wk, (SUBLANES, MLSTM_CHUNK)).astype(BF16)
    n_ref[h] = decay * n_ref[h] + _dot(wk_rows, t["ks"])
    m_ref[h] = jnp.broadcast_to(m_new, (SUBLANES, LANES))


def _mlstm_kernel(*refs):
    n_in, n_state = 5, 3
    ins = (refs[:n_in], refs[n_in:2 * n_in])
    outs = (refs[2 * n_in], refs[2 * n_in + 1])
    state = refs[2 * n_in + 2:]
    states = (state[:n_state], state[n_state:])

    @pl.when(pl.program_id(1) == 0)
    def _():
        for ref in state:
            ref[...] = jnp.zeros_like(ref)

    gates = [_mlstm_gates(ins[d][3], ins[d][4], reverse=bool(d)) for d in range(2)]
    chains = [(d, h) for h in range(A_HEADS) for d in range(2)]
    terms = {}
    for d, h in chains:
        q_ref, k_ref = ins[d][0], ins[d][1]
        terms[d, h] = _mlstm_scores(gates[d], h, bool(d), q_ref, k_ref, states[d][2])
    for d, h in chains:
        ct_ref, n_ref, _ = states[d]
        _mlstm_output(terms[d, h], h, ins[d][2], outs[d], ct_ref, n_ref)
    for d, h in chains:
        _mlstm_state(terms[d, h], h, *states[d])


def _mlstm(proj, gates, gates_t):
    bsz = proj.shape[0]
    n_lat_chunks = N_LAT // MLSTM_CHUNK

    def chunk_of(step, reverse):
        latent = (n_lat_chunks - step) if reverse else (step - 1)
        return jnp.where(step == 0, n_lat_chunks, latent)

    def specs(reverse):
        ck = functools.partial(chunk_of, reverse=reverse)
        return [
            pl.BlockSpec((1, MLSTM_CHUNK, A_HEADS * A_QK), lambda b, s: (b, ck(s), 0)),
            pl.BlockSpec((1, MLSTM_CHUNK, A_HEADS * A_QK), lambda b, s: (b, ck(s), 1)),
            pl.BlockSpec((1, MLSTM_CHUNK, D_MODEL), lambda b, s: (b, ck(s), COL_AV)),
            pl.BlockSpec((1, MLSTM_CHUNK, LANES), lambda b, s: (b, ck(s), 0)),
            pl.BlockSpec((1, N_GATE, MLSTM_CHUNK), lambda b, s: (b, 0, ck(s))),
        ]

    def out_spec(reverse):
        ck = functools.partial(chunk_of, reverse=reverse)
        return pl.BlockSpec((1, MLSTM_CHUNK, D_MODEL), lambda b, s: (b, ck(s), 0))

    state = [pltpu.VMEM((A_HEADS, A_V, A_QK), F32),
             pltpu.VMEM((A_HEADS, SUBLANES, A_QK), F32),
             pltpu.VMEM((A_HEADS, SUBLANES, LANES), F32)]
    args = (proj, proj, proj, gates, gates_t)
    return pl.pallas_call(
        _mlstm_kernel,
        grid=(bsz, N_TOK // MLSTM_CHUNK),
        in_specs=specs(False) + specs(True),
        out_specs=[out_spec(False), out_spec(True)],
        out_shape=[jax.ShapeDtypeStruct((bsz, N_TOK, D_MODEL), F32)] * 2,
        scratch_shapes=state + state,
        compiler_params=_cparams("parallel", "arbitrary"),
        name="mlstm_bidirectional",
    )(*args, *args)


def _mlstm_out_kernel(hf_ref, hb_ref, o_ref, z_ref, w_ref, y_ref):
    for h in range(A_HEADS):
        cols = slice(h * A_V, (h + 1) * A_V)
        x = hf_ref[0, :, cols] + hb_ref[0, :, cols]
        xn = x * lax.rsqrt(jnp.mean(x * x, axis=-1, keepdims=True) + EPS) * w_ref[:, cols]
        gate = _sigmoid(o_ref[0, :, cols].astype(F32)) * _silu(z_ref[0, :, cols].astype(F32))
        y_ref[0, :, cols] = (xn * gate).astype(y_ref.dtype)


def _mlstm_out(h_fwd, h_bwd, proj, a_norm_w):
    bsz = proj.shape[0]
    col = lambda c: pl.BlockSpec((1, TOK_TILE, D_MODEL), lambda b, t: (b, t, c))
    return pl.pallas_call(
        _mlstm_out_kernel,
        grid=(bsz, N_TOK // TOK_TILE),
        in_specs=[col(0), col(0), col(COL_AO), col(COL_AZ),
                  pl.BlockSpec((1, D_MODEL), lambda b, t: (0, 0))],
        out_specs=col(0),
        out_shape=jax.ShapeDtypeStruct((bsz, N_TOK, D_MODEL), BF16),
        compiler_params=_cparams("parallel", "parallel"),
        name="mlstm_out",
    )(h_fwd, h_bwd, proj, proj, a_norm_w)


def _chan_dft_kernel(u_ref, cs_ref, lat_ref, ctx_ref, p_ref, q_ref):
    t = pl.program_id(1)
    n_lane = C_GROUP_DIM // LANES

    def channel_dft(n_rows):
        for g in range(C_GROUPS):
            r = _dot(u_ref[0, :n_rows * GRID_W, g * C_GROUP_DIM:(g + 1) * C_GROUP_DIM], cs_ref[...])
            for i in range(n_rows):
                src = slice(i * GRID_W, (i + 1) * GRID_W)
                dst = slice(i * CH_PITCH, i * CH_PITCH + GRID_W)
                for j in range(n_lane):
                    p_ref[g * n_lane + j, dst, :] = r[src, j * LANES:(j + 1) * LANES]
                    q_ref[g * n_lane + j, dst, :] = r[src, C_GROUP_DIM + j * LANES:C_GROUP_DIM + (j + 1) * LANES]

    @pl.when(t < pl.num_programs(1) - 1)
    def _():
        channel_dft(CH_ROWS)
        for c in range(GRID_W):
            rows = pl.ds(c, CH_ROWS, stride=CH_PITCH)
            for j in range(D_MODEL // LANES):
                lanes = slice(c * D_MODEL + j * LANES, c * D_MODEL + (j + 1) * LANES)
                lat_ref[0, 0, :, lanes] = p_ref[j, rows, :].astype(BF16)
                lat_ref[0, 1, :, lanes] = q_ref[j, rows, :].astype(BF16)

    @pl.when(t == pl.num_programs(1) - 1)
    def _():
        n_rows = N_CTX // GRID_W
        channel_dft(n_rows)
        for i in range(n_rows):
            src = slice(i * CH_PITCH, i * CH_PITCH + GRID_W)
            dst = slice(i * GRID_W, (i + 1) * GRID_W)
            for j in range(D_MODEL // LANES):
                lanes = slice(j * LANES, (j + 1) * LANES)
                ctx_ref[0, 0, dst, lanes] = p_ref[j, src, :].astype(BF16)
                ctx_ref[0, 1, dst, lanes] = q_ref[j, src, :].astype(BF16)


def _chan_dft(proj, cs):
    bsz = proj.shape[0]
    tile = CH_ROWS * GRID_W
    n_lat = N_LAT // tile
    return pl.pallas_call(
        _chan_dft_kernel,
        grid=(bsz, n_lat + 1),
        in_specs=[pl.BlockSpec((1, tile, D_MODEL), lambda b, t: (b, t, COL_CU)),
                  pl.BlockSpec((C_GROUP_DIM, 2 * C_GROUP_DIM), lambda b, t: (0, 0))],
        out_specs=[pl.BlockSpec((1, 2, CH_ROWS, GRID_W * D_MODEL),
                                lambda b, t: (b, 0, jnp.minimum(t, n_lat - 1), 0)),
                   pl.BlockSpec((1, 2, N_CTX, D_MODEL), lambda b, t: (b, 0, 0, 0))],
        out_shape=[jax.ShapeDtypeStruct((bsz, 2, GRID_W, GRID_W * D_MODEL), BF16),
                   jax.ShapeDtypeStruct((bsz, 2, N_CTX, D_MODEL), BF16)],
        scratch_shapes=[pltpu.VMEM((D_MODEL // LANES, CH_ROWS * CH_PITCH, LANES), F32)] * 2,
        compiler_params=_cparams("parallel", "arbitrary"),
        name="fourier_channel_dft",
    )(proj, cs)


def _row_dft_kernel(pq_ref, cs_ref, tc_ref, ts_ref, z_ref):
    cos, sin = cs_ref[0], cs_ref[1]
    reps = D_MODEL // LANES
    for j in range(FFT_COLS):
        cols = slice(j * D_MODEL, (j + 1) * D_MODEL)
        p, q = pq_ref[0, 0, :, cols], pq_ref[0, 1, :, cols]
        a_re = _dot(cos, p) - _dot(sin, q)
        a_im = _dot(cos, q) + _dot(sin, p)
        tc = jnp.tile(tc_ref[j], (1, reps))
        ts = jnp.tile(ts_ref[j], (1, reps))
        z_ref[0, 0, :, cols] = (a_re * tc - a_im * ts).astype(z_ref.dtype)
        z_ref[0, 1, :, cols] = (a_re * ts + a_im * tc).astype(z_ref.dtype)


def _col_dft_kernel(z_ref, cs_ref, g_ref, y_in_ref, y_ref):
    del y_in_ref
    rows = FFT_K1 * GRID_W
    z_re = z_ref[0, 0].reshape(rows, D_MODEL)
    z_im = z_ref[0, 1].reshape(rows, D_MODEL)
    y = _dot(cs_ref[0], z_re) - _dot(cs_ref[1], z_im)
    g = g_ref[0].reshape(rows, D_MODEL).astype(F32)
    y_ref[0] = (y * _silu(g)).astype(y_ref.dtype).reshape(GRID_W, FFT_K1, D_MODEL)


def _seq_dft_latent(pq, proj, tables, y_prev):
    bsz = proj.shape[0]
    side = GRID_W
    cs_a, cs_b, tw_c, tw_s = tables
    n_rows_all = N_TOK // side
    full = lambda shape: pl.BlockSpec(shape, lambda b, t: (0,) * len(shape))
    z = pl.pallas_call(
        _row_dft_kernel,
        grid=(bsz, side // FFT_COLS),
        in_specs=[pl.BlockSpec((1, 2, side, FFT_COLS * D_MODEL), lambda b, t: (b, 0, 0, t)),
                  full((2, side, side)),
                  pl.BlockSpec((FFT_COLS, side, LANES), lambda b, t: (t, 0, 0)),
                  pl.BlockSpec((FFT_COLS, side, LANES), lambda b, t: (t, 0, 0))],
        out_specs=pl.BlockSpec((1, 2, side, FFT_COLS * D_MODEL), lambda b, t: (b, 0, 0, t)),
        out_shape=jax.ShapeDtypeStruct((bsz, 2, side, side * D_MODEL), BF16),
        compiler_params=_cparams("parallel", "parallel"),
        name="fourier_row_dft",
    )(pq, cs_a, tw_c, tw_s)
    y = pl.pallas_call(
        _col_dft_kernel,
        grid=(bsz, side // FFT_K1),
        in_specs=[pl.BlockSpec((1, 2, FFT_K1, side, D_MODEL), lambda b, t: (b, 0, t, 0, 0)),
                  full((2, FFT_K1 * side, FFT_K1 * side)),
                  pl.BlockSpec((1, side, FFT_K1, D_MODEL), lambda b, t: (b, 0, t, COL_CZ)),
                  pl.BlockSpec(memory_space=pl.ANY)],
        out_specs=pl.BlockSpec((1, side, FFT_K1, D_MODEL), lambda b, t: (b, 0, t, 0)),
        out_shape=jax.ShapeDtypeStruct((bsz, n_rows_all, side, D_MODEL), BF16),
        input_output_aliases={3: 0},
        compiler_params=_cparams("parallel", "parallel"),
        name="fourier_col_dft",
    )(z.reshape(bsz, 2, side, side, D_MODEL), cs_b, proj.reshape(bsz, n_rows_all, side, N_PROJ),
      y_prev.reshape(bsz, n_rows_all, side, D_MODEL))
    return y.reshape(bsz, N_TOK, D_MODEL)


def _seq_dft_ctx_kernel(a_ref, pq_ref, z_ref, y_in_ref, y_ref):
    del y_in_ref
    r = _dot(a_ref[:, :N_CTX], pq_ref[0, 0]) + _dot(a_ref[:, N_CTX:], pq_ref[0, 1])
    y_ref[0] = (r * _silu(z_ref[0].astype(F32))).astype(y_ref.dtype)


def _seq_dft_ctx(dft_ctx, pq, proj, y_c):
    bsz = proj.shape[0]
    ctx_blk = N_LAT // N_CTX
    return pl.pallas_call(
        _seq_dft_ctx_kernel,
        grid=(bsz,),
        in_specs=[
            pl.BlockSpec((N_CTX, 2 * N_CTX), lambda b: (0, 0)),
            pl.BlockSpec((1, 2, N_CTX, D_MODEL), lambda b: (b, 0, 0, 0)),
            pl.BlockSpec((1, N_CTX, D_MODEL), lambda b: (b, ctx_blk, COL_CZ)),
            pl.BlockSpec(memory_space=pl.ANY),
        ],
        out_specs=pl.BlockSpec((1, N_CTX, D_MODEL), lambda b: (b, ctx_blk, 0)),
        out_shape=jax.ShapeDtypeStruct(y_c.shape, y_c.dtype),
        input_output_aliases={3: 0},
        compiler_params=_cparams("parallel"),
        name="fourier_token_dft_ctx",
    )(dft_ctx, pq, proj, y_c)


def _merge_kernel(x_ref, ya_ref, yb_ref, yc_ref, ga_ref, gb_ref, gc_ref, mod_ref,
                  wa_ref, wb_ref, wc_ref, wo_ref, o_ref, *, tile):
    b = pl.program_id(0)

    def branch(y_ref, g_ref, w_ref):
        return _sigmoid(g_ref[0].astype(F32)) * _dot(y_ref[0], w_ref[...])

    y = branch(ya_ref, ga_ref, wa_ref) + branch(yb_ref, gb_ref, wb_ref) + branch(yc_ref, gc_ref, wc_ref)
    out = _dot(y.astype(BF16), wo_ref[...])
    gate_cols = slice(2 * D_MODEL, 3 * D_MODEL)
    gate = jnp.where(_ctx_rows(pl.program_id(1), tile),
                     mod_ref[pl.ds(B_CTX_ROW, 1), gate_cols], mod_ref[pl.ds(b, 1), gate_cols])
    o_ref[0] = x_ref[0] + gate * out


def _merge(xs, y_a, y_b, y_c, proj, mod_l, wa, wb, wc, wo, n_rows, tile):
    bsz = xs.shape[0]
    col = lambda c: pl.BlockSpec((1, tile, D_MODEL), lambda b, t: (b, t, c))
    weight = pl.BlockSpec((D_MODEL, D_MODEL), lambda b, t: (0, 0))
    return pl.pallas_call(
        functools.partial(_merge_kernel, tile=tile),
        grid=(bsz, n_rows // tile),
        in_specs=[col(0), col(0), col(0), col(0), col(COL_MG), col(COL_MG + 1), col(COL_MG + 2),
                  pl.BlockSpec((SUBLANES, 3 * D_MODEL), lambda b, t: (0, 0)),
                  weight, weight, weight, weight],
        out_specs=col(0),
        out_shape=jax.ShapeDtypeStruct((bsz, n_rows, D_MODEL), F32),
        compiler_params=_cparams("parallel", "parallel"),
        name="merge_out_residual",
    )(xs, y_a, y_b, y_c, proj, proj, proj, mod_l, wa, wb, wc, wo)


def _rope_tables():
    n = jnp.arange(N_LAT, dtype=jnp.int32)
    row = (n // GRID_W).astype(F32)
    col = (n % GRID_W).astype(F32)
    inv_freq = ROPE_THETA ** (-jnp.arange(0, ROPE_AXIS_DIM, 2, dtype=F32) / ROPE_AXIS_DIM)
    ang_r = row[:, None] * inv_freq
    ang_c = col[:, None] * inv_freq
    cos = jnp.concatenate([jnp.cos(ang_r)] * 2 + [jnp.cos(ang_c)] * 2, axis=-1)
    sin = jnp.concatenate([-jnp.sin(ang_r), jnp.sin(ang_r), -jnp.sin(ang_c), jnp.sin(ang_c)], axis=-1)
    cos = jnp.concatenate([cos, jnp.ones((N_CTX, B_QK), F32)], axis=0)
    sin = jnp.concatenate([sin, jnp.zeros((N_CTX, B_QK), F32)], axis=0)
    reps = LANES // B_QK
    return jnp.tile(cos, (1, reps)), jnp.tile(sin, (1, reps))


def _dft_cos_sin(n, scale):
    idx = jnp.arange(n, dtype=jnp.int32)
    ang = ((idx[:, None] * idx[None, :]) % n).astype(F32) * (2.0 * math.pi / n)
    return jnp.cos(ang) * scale, jnp.sin(ang) * scale


def _dft_tables():
    c_ch, s_ch = _dft_cos_sin(C_GROUP_DIM, 1.0)
    chan = jnp.concatenate([c_ch, s_ch], axis=1).astype(BF16)
    c_c, s_c = _dft_cos_sin(N_CTX, (N_CTX * C_GROUP_DIM) ** -0.5)
    ctx = jnp.concatenate([c_c, -s_c], axis=1).astype(BF16)
    c_s, s_s = _dft_cos_sin(GRID_W, 1.0)
    stage_a = jnp.stack([c_s, s_s]).astype(BF16)
    eye = jnp.eye(FFT_K1, dtype=F32)
    expand = lambda t: jnp.einsum("kn,ij->kijn", t, eye).reshape(
        FFT_K1 * GRID_W, FFT_K1 * GRID_W)
    stage_b = (jnp.stack([expand(c_s), expand(s_s)]) * (N_LAT * C_GROUP_DIM) ** -0.5).astype(BF16)
    idx = jnp.arange(GRID_W, dtype=jnp.int32)
    ang = (idx[:, None] * idx[None, :]).astype(F32) * (2.0 * math.pi / N_LAT)
    lanes = lambda t: jnp.broadcast_to(t[:, :, None], (GRID_W, GRID_W, LANES))
    return chan, ctx, (stage_a, stage_b, lanes(jnp.cos(ang)), lanes(jnp.sin(ang)))


def _group_sum_matrix():
    g = np.arange(LANES) // B_QK
    return jnp.asarray(g[:, None] == g[None, :], dtype=BF16)


def kernel(x, c, ctx, c_ctx, norm_w, w_ada, b_ada, w_in, b_if, a_norm_w, q_norm_w, k_norm_w,
           lambda_q1, lambda_k1, lambda_q2, lambda_k2, subln_w, w_a_out, w_b_out, w_c_out, w_out):
    bsz = x.shape[0]
    xs = jnp.concatenate([x, ctx], axis=1)
    c_rows = jnp.concatenate(
        [c, c_ctx[None, :], jnp.zeros((SUBLANES - bsz - 1, D_MODEL), F32)], axis=0)
    mod, lam = _modulation(c_rows, w_ada, b_ada, lambda_q1, lambda_k1, lambda_q2, lambda_k2)

    cos_t, sin_t = _rope_tables()
    dft_chan, dft_ctx, dft_lat = _dft_tables()
    group_sum = _group_sum_matrix()

    w_in_t = jnp.swapaxes(w_in, 1, 2)
    w_tail_t = w_in_t[:, w_in_t.shape[1] - N_GATE:, :]
    lane_pad = LANES - N_GATE
    pad_lanes = lambda w: jnp.pad(w, ((0, 0),) * (w.ndim - 1) + ((0, lane_pad),))
    w_gate = pad_lanes(w_in[:, :, GATE_COL0:GATE_COL0 + N_GATE])
    b_gate = pad_lanes(b_if)
    wa, wb, wc, wo = (w.astype(BF16) for w in (w_a_out, w_b_out, w_c_out, w_out))
    tile2 = lambda w: jnp.tile(w[None, :], (1, LANES // B_QK))

    for l in range(DEPTH):
        last = l == DEPTH - 1
        h, gates = _norm_mod(xs, norm_w[l][None, :], mod[l], w_gate[l], b_gate[l][None, :])
        proj = _in_projection(h.reshape(bsz * N_TOK, D_MODEL), w_in_t, w_tail_t, l).reshape(bsz, N_TOK, N_PROJ)

        gates_t = jnp.swapaxes(gates[:, :, :N_GATE], 1, 2)
        h_fwd, h_bwd = _mlstm(proj, gates, gates_t)
        y_a = _mlstm_out(h_fwd, h_bwd, proj, a_norm_w[l][None, :])

        qh, kt = _qk_prep(proj, cos_t, sin_t, tile2(q_norm_w[l]), tile2(k_norm_w[l]), group_sum)
        lam_l = lam[l, 0, :1]
        lam_init = 0.8 - 0.6 * math.exp(-0.3 * l)
        y_b = _diff_attention(lam_l, qh, kt, proj, subln_w[l][None, :], lam_init, h, ctx=False)
        if not last:
            y_b = _diff_attention(lam_l, qh, kt, proj, subln_w[l][None, :], lam_init, y_b, ctx=True)

        pq, pq_ctx = _chan_dft(proj, dft_chan)
        y_c = _seq_dft_latent(pq, proj, dft_lat, qh)

        if last:
            return _merge(xs, y_a, y_b, y_c, proj, mod[l], wa[l], wb[l], wc[l], wo[l],
                          n_rows=N_LAT, tile=LAST_TILE)
        y_c = _seq_dft_ctx(dft_ctx, pq_ctx, proj, y_c)
        xs = _merge(xs, y_a, y_b, y_c, proj, mod[l], wa[l], wb[l], wc[l], wo[l],
                    n_rows=N_TOK, tile=TOK_TILE)
```

```python
import functools
import math

import jax
import jax.numpy as jnp
import numpy as np
from jax import lax
from jax.experimental import pallas as pl
from jax.experimental.pallas import tpu as pltpu

F32 = jnp.float32
BF16 = jnp.bfloat16
LOG2E = math.log2(math.e)

D_MODEL = 1024
DEPTH = 4
N_LAT = 4096
N_CTX = 256
N_TOK = N_LAT + N_CTX
GRID_W = 64
EPS = 1e-6

A_HEADS, A_QK, A_V = 4, 128, 256
B_HEADS, B_QK, B_V = 8, 64, 128
C_GROUPS, C_GROUP_DIM = 4, 256
ROPE_AXIS_DIM = B_QK // 2
ROPE_THETA = 10000.0
N_GATE = 4 * A_HEADS
BATCH = 4
B_CTX_ROW = BATCH

COL_AQK, COL_AV, COL_AO, COL_AZ, COL_BQ, COL_BK, COL_BV, COL_BZ, COL_CU, COL_CZ, COL_MG = (
    0, 1, 2, 3, 4, 5, 6, 7, 8, 9, 10)
N_PROJ = 13 * D_MODEL
GATE_COL0 = 2 * A_HEADS * A_QK + A_HEADS * A_V

LANES = 128
SUBLANES = 8
VMEM_LIMIT_BYTES = 56 * 1024 * 1024

TOK_TILE = 544
MM_TM = 2176
MM_TN = 1024
ATT_TQ = 256
ATT_ROWS = 2 * ATT_TQ
QK_TILE = 256
MERGE_TILE = 272
LAST_TILE = 256
MLSTM_CHUNK = 256
CH_ROWS = 16
CH_PITCH = GRID_W + 8
FFT_COLS = 8
FFT_K1 = 16


def _cparams(*sem):
    return pltpu.CompilerParams(dimension_semantics=sem, vmem_limit_bytes=VMEM_LIMIT_BYTES)


def _split3(x):
    hi = x.astype(BF16)
    r1 = x - hi.astype(F32)
    mid = r1.astype(BF16)
    lo = (r1 - mid.astype(F32)).astype(BF16)
    return hi, mid, lo


def _dot(a, b):
    return jnp.dot(a, b, preferred_element_type=F32)


def _dot_f32_rhs(a_exact, b):
    hi, mid, lo = _split3(b)
    return _dot(a_exact, hi) + _dot(a_exact, mid) + _dot(a_exact, lo)


def _dot_f32_lhs(a, b_exact):
    hi, mid, lo = _split3(a)
    return _dot(hi, b_exact) + _dot(mid, b_exact) + _dot(lo, b_exact)


def _dot_2pass_lhs(a, b_exact):
    hi = a.astype(BF16)
    mid = (a - hi.astype(F32)).astype(BF16)
    return _dot(hi, b_exact) + _dot(mid, b_exact)


def _dot_f32(a, b):
    a_hi = a.astype(BF16)
    a_lo = (a - a_hi.astype(F32)).astype(BF16)
    b_hi = b.astype(BF16)
    b_lo = (b - b_hi.astype(F32)).astype(BF16)
    return _dot(a_hi, b_hi) + _dot(a_lo, b_hi) + _dot(a_hi, b_lo)


def _sigmoid(x):
    return 1.0 / (1.0 + jnp.exp(-x))


def _silu(x):
    return x * _sigmoid(x)


def _log_sigmoid(x):
    return jnp.minimum(x, 0.0) - jnp.log(1.0 + jnp.exp(-jnp.abs(x)))


def _mod_kernel(c_ref, w_ref, b_ref, q1_ref, k1_ref, q2_ref, k2_ref, mod_ref, lam_ref):
    sc = _silu(c_ref[...])
    mod_ref[0] = _dot_f32(sc, w_ref[0]) + b_ref[0]
    layer = jnp.full((1, LANES), pl.program_id(0), dtype=jnp.int32).astype(F32)
    lam_init = 0.8 - 0.6 * jnp.exp(-0.3 * layer)
    s1 = jnp.sum(q1_ref[0] * k1_ref[0], axis=-1, keepdims=True)
    s2 = jnp.sum(q2_ref[0] * k2_ref[0], axis=-1, keepdims=True)
    lam_ref[0] = jnp.exp(s1) - jnp.exp(s2) + lam_init


def _modulation(c_rows, w_ada, b_ada, lq1, lk1, lq2, lk2):
    n_col = 3
    vec = lambda a: a.reshape(DEPTH, 1, B_QK)
    lam_spec = pl.BlockSpec((1, 1, B_QK), lambda l, j: (l, 0, 0))
    return pl.pallas_call(
        _mod_kernel,
        grid=(DEPTH, n_col),
        in_specs=[
            pl.BlockSpec((SUBLANES, D_MODEL), lambda l, j: (0, 0)),
            pl.BlockSpec((1, D_MODEL, D_MODEL), lambda l, j: (l, 0, j)),
            pl.BlockSpec((1, 1, D_MODEL), lambda l, j: (l, 0, j)),
            lam_spec, lam_spec, lam_spec, lam_spec,
        ],
        out_specs=[
            pl.BlockSpec((1, SUBLANES, D_MODEL), lambda l, j: (l, 0, j)),
            pl.BlockSpec((1, 1, LANES), lambda l, j: (l, 0, 0)),
        ],
        out_shape=[
            jax.ShapeDtypeStruct((DEPTH, SUBLANES, 3 * D_MODEL), F32),
            jax.ShapeDtypeStruct((DEPTH, 1, LANES), F32),
        ],
        compiler_params=_cparams("arbitrary", "arbitrary"),
        name="adaln_modulation",
    )(c_rows, w_ada, b_ada.reshape(DEPTH, 1, 3 * D_MODEL), vec(lq1), vec(lk1), vec(lq2), vec(lk2))


def _ctx_rows(tile_idx, tile):
    row = tile_idx * tile + lax.broadcasted_iota(jnp.int32, (tile, 1), 0)
    return row >= N_LAT


def _norm_mod_kernel(x_ref, nw_ref, mod_ref, wg_ref, bif_ref, h_ref, g_ref):
    b = pl.program_id(0)
    x = x_ref[0]
    y = x * lax.rsqrt(jnp.mean(x * x, axis=-1, keepdims=True) + EPS) * nw_ref[...]
    is_ctx = _ctx_rows(pl.program_id(1), TOK_TILE)
    m_lat = mod_ref[pl.ds(b, 1), :]
    m_ctx = mod_ref[pl.ds(B_CTX_ROW, 1), :]
    shift = jnp.where(is_ctx, m_ctx[:, 0:D_MODEL], m_lat[:, 0:D_MODEL])
    scale = jnp.where(is_ctx, m_ctx[:, D_MODEL:2 * D_MODEL], m_lat[:, D_MODEL:2 * D_MODEL])
    h = y * (1.0 + scale) + shift
    h_ref[0] = h.astype(BF16)
    g_ref[0] = _dot_f32(h, wg_ref[...]) + bif_ref[...]


def _norm_mod(xs, norm_w, mod_l, w_gate, b_if):
    bsz = xs.shape[0]
    tok = pl.BlockSpec((1, TOK_TILE, D_MODEL), lambda b, t: (b, t, 0))
    full = lambda shape: pl.BlockSpec(shape, lambda b, t: (0,) * len(shape))
    return pl.pallas_call(
        _norm_mod_kernel,
        grid=(bsz, N_TOK // TOK_TILE),
        in_specs=[tok, full((1, D_MODEL)), full((SUBLANES, 3 * D_MODEL)),
                  full((D_MODEL, LANES)), full((1, LANES))],
        out_specs=[tok, pl.BlockSpec((1, TOK_TILE, LANES), lambda b, t: (b, t, 0))],
        out_shape=[jax.ShapeDtypeStruct((bsz, N_TOK, D_MODEL), BF16),
                   jax.ShapeDtypeStruct((bsz, N_TOK, LANES), F32)],
        compiler_params=_cparams("parallel", "parallel"),
        name="norm_modulate",
    )(xs, norm_w, mod_l, w_gate, b_if)


def _in_proj_kernel(a_ref, w0_ref, w1_ref, tail_ref, o_ref, wb_ref):
    j = pl.program_id(0)
    n_plain = GATE_COL0 // MM_TN

    @pl.when((pl.program_id(1) == 0) & (j < n_plain))
    def _():
        wb_ref[...] = w0_ref[0].astype(BF16)

    @pl.when((pl.program_id(1) == 0) & (j >= n_plain))
    def _():
        nxt = jnp.where(j == pl.num_programs(0) - 1, tail_ref[0], w1_ref[0, :N_GATE, :])
        wb_ref[:MM_TN - N_GATE, :] = w0_ref[0, N_GATE:, :].astype(BF16)
        wb_ref[MM_TN - N_GATE:, :] = nxt.astype(BF16)

    o_ref[...] = lax.dot_general(a_ref[...], wb_ref[...], (((1,), (1,)), ((), ())),
                                 preferred_element_type=F32).astype(o_ref.dtype)


def _in_projection(h2d, w_in_t, w_tail_t, layer):
    m, k = h2d.shape
    n_col = N_PROJ // MM_TN
    return pl.pallas_call(
        _in_proj_kernel,
        grid=(n_col, m // MM_TM),
        in_specs=[pl.BlockSpec((MM_TM, k), lambda j, i: (i, 0)),
                  pl.BlockSpec((1, MM_TN, k), lambda j, i: (layer, j, 0)),
                  pl.BlockSpec((1, MM_TN, k), lambda j, i: (layer, jnp.minimum(j + 1, n_col - 1), 0)),
                  pl.BlockSpec((1, N_GATE, k), lambda j, i: (layer, 0, 0))],
        out_specs=pl.BlockSpec((MM_TM, MM_TN), lambda j, i: (i, j)),
        out_shape=jax.ShapeDtypeStruct((m, N_PROJ), BF16),
        scratch_shapes=[pltpu.VMEM((MM_TN, k), BF16)],
        compiler_params=_cparams("arbitrary", "arbitrary"),
        name="in_projection",
    )(h2d, w_in_t, w_in_t, w_tail_t)


def _qk_prep_kernel(q_ref, k_ref, cos_ref, sin_ref, qw_ref, kw_ref, g_ref, qo_ref, kt_ref):
    cos = cos_ref[...]
    sin = sin_ref[...]
    group_mean = g_ref[...]
    lane = lax.broadcasted_iota(jnp.int32, (1, LANES), 1)
    first_half = (lane % ROPE_AXIS_DIM) < (ROPE_AXIS_DIM // 2)

    def prep(x_bf16, w):
        x = x_bf16.astype(F32)
        ms = _dot_2pass_lhs(x * x, group_mean)
        y = x * lax.rsqrt(ms + EPS) * w
        half = ROPE_AXIS_DIM // 2
        partner = jnp.where(first_half, pltpu.roll(y, LANES - half, 1), pltpu.roll(y, half, 1))
        return y * cos + partner * sin

    for s in range(D_MODEL // LANES):
        cols = slice(s * LANES, (s + 1) * LANES)
        qo_ref[0, :, cols] = (prep(q_ref[0, :, cols], qw_ref[...]) * (B_QK ** -0.5 * LOG2E)).astype(BF16)
        kt_ref[0, cols, :] = prep(k_ref[0, :, cols], kw_ref[...]).T.astype(BF16)


def _qk_prep(proj, cos_t, sin_t, qw, kw, group_mean):
    bsz = proj.shape[0]
    col = lambda c: pl.BlockSpec((1, QK_TILE, D_MODEL), lambda b, t: (b, t, c))
    tab = pl.BlockSpec((QK_TILE, LANES), lambda b, t: (t, 0))
    full = lambda shape: pl.BlockSpec(shape, lambda b, t: (0,) * len(shape))
    return pl.pallas_call(
        _qk_prep_kernel,
        grid=(bsz, N_TOK // QK_TILE),
        in_specs=[col(COL_BQ), col(COL_BK), tab, tab, full((1, LANES)), full((1, LANES)),
                  full((LANES, LANES))],
        out_specs=[pl.BlockSpec((1, QK_TILE, D_MODEL), lambda b, t: (b, t, 0)),
                   pl.BlockSpec((1, D_MODEL, QK_TILE), lambda b, t: (b, 0, t))],
        out_shape=[jax.ShapeDtypeStruct((bsz, N_TOK, D_MODEL), BF16),
                   jax.ShapeDtypeStruct((bsz, D_MODEL, N_TOK), BF16)],
        compiler_params=_cparams("parallel", "parallel"),
        name="qk_norm_rope",
    )(proj, proj, cos_t, sin_t, qw, kw, group_mean)


def _diff_attn_body(lam_ref, q_ref, kt_ref, v_ref, z_ref, sw_ref, o_ref, vp_ref, lam_init):
    @pl.when(pl.program_id(2) == 0)
    def _():
        vp_ref[:, :B_V] = v_ref[0]
        vp_ref[:, B_V:] = jnp.ones((vp_ref.shape[0], B_V), BF16)

    chains = [slice(c * ATT_TQ, (c + 1) * ATT_TQ) for c in range(q_ref.shape[1] // ATT_TQ)]
    lane = lax.broadcasted_iota(jnp.int32, (ATT_TQ, B_V), 1)
    zero = jnp.zeros((ATT_TQ, B_V), BF16)

    def scores(rows, first_map):
        q = q_ref[0, rows, :]
        return _dot(jnp.where((lane < B_QK) if first_map else (lane >= B_QK), q, zero), kt_ref[0])

    def numerators(s):
        return jnp.exp2(s - jnp.max(s, axis=-1, keepdims=True)).astype(BF16)

    s0 = [scores(rows, True) for rows in chains]
    s1 = [scores(rows, False) for rows in chains]
    u0 = [_dot(numerators(s), vp_ref[...]) for s in s0]
    u1 = [_dot(numerators(s), vp_ref[...]) for s in s1]
    for rows, a, b in zip(chains, u0, u1):
        o = a[:, :B_V] / a[:, B_V:] - lam_ref[0] * (b[:, :B_V] / b[:, B_V:])
        y = o * lax.rsqrt(jnp.mean(o * o, axis=-1, keepdims=True) + EPS) * sw_ref[...]
        o_ref[0, rows, :] = (y * (1.0 - lam_init)
                             * _silu(z_ref[0, rows, :].astype(F32))).astype(o_ref.dtype)


def _diff_attn_kernel(lam_ref, q_ref, kt_ref, v_ref, z_ref, sw_ref, y_in_ref, o_ref, vp_ref,
                      *, lam_init):
    del y_in_ref
    _diff_attn_body(lam_ref, q_ref, kt_ref, v_ref, z_ref, sw_ref, o_ref, vp_ref, lam_init)


def _diff_attention(lam, qh, kt, proj, subln_w, lam_init, y_prev, ctx):
    bsz = qh.shape[0]
    heads_per_block = D_MODEL // B_V
    n_keys = N_CTX if ctx else N_TOK
    rows = N_CTX if ctx else ATT_ROWS
    q0 = N_LAT // rows if ctx else 0
    k0 = N_LAT // N_CTX if ctx else 0
    tile = lambda c0: pl.BlockSpec((1, rows, B_V), lambda b, h, i: (b, q0 + i, c0 + h))
    in_specs = [pl.BlockSpec(memory_space=pltpu.SMEM),
                tile(0),
                pl.BlockSpec((1, B_V, n_keys), lambda b, h, i: (b, h, k0)),
                pl.BlockSpec((1, n_keys, B_V), lambda b, h, i: (b, k0, COL_BV * heads_per_block + h)),
                tile(COL_BZ * heads_per_block),
                pl.BlockSpec((1, B_V), lambda b, h, i: (0, 0)),
                pl.BlockSpec(memory_space=pl.ANY)]
    return pl.pallas_call(
        functools.partial(_diff_attn_kernel, lam_init=lam_init),
        grid=(bsz, B_HEADS, (N_CTX if ctx else N_LAT) // rows),
        in_specs=in_specs,
        out_specs=tile(0),
        out_shape=jax.ShapeDtypeStruct((bsz, N_TOK, D_MODEL), BF16),
        scratch_shapes=[pltpu.VMEM((n_keys, 2 * B_V), BF16)],
        input_output_aliases={6: 0},
        compiler_params=_cparams("parallel", "parallel", "arbitrary"),
        name="diff_attention_ctx" if ctx else "diff_attention",
    )(lam, qh, kt, proj, proj, subln_w, y_prev)


def _mlstm_gates(gc_ref, gr_ref, reverse):
    chunk = MLSTM_CHUNK
    row = lax.broadcasted_iota(jnp.int32, (chunk, chunk), 0)
    col = lax.broadcasted_iota(jnp.int32, (chunk, chunk), 1)
    seen = (row >= col) if reverse else (row <= col)
    vis_ts = jnp.where((row <= col) if reverse else (row >= col), 1.0, 0.0).astype(BF16)
    vis_st = jnp.where(seen, 1.0, 0.0).astype(BF16)
    gate_c = gc_ref[0]
    gate_r = gr_ref[0]
    logf_r = _log_sigmoid(gate_r)
    return dict(seen=seen, gate_c=gate_c, gate_r=gate_r, logf_r=logf_r,
                cum_c=_dot_f32_rhs(vis_ts, _log_sigmoid(gate_c)),
                cum_r=_dot_f32_lhs(logf_r, vis_st))


def _mlstm_scores(g, h, reverse, q_ref, k_ref, m_ref):
    d = 1 if reverse else 0
    i_idx = (2 * d) * A_HEADS + h
    f_idx = (2 * d + 1) * A_HEADS + h
    src = g["gate_c"][:, i_idx:i_idx + 1] - g["cum_c"][:, f_idx:f_idx + 1]
    ic_r = g["gate_r"][i_idx:i_idx + 1, :]
    b_r = g["cum_r"][f_idx:f_idx + 1, :]
    m_st = m_ref[h, 0:1, 0:1]
    q_t = q_ref[0, :, h * A_QK:(h + 1) * A_QK].astype(F32).T.astype(BF16)
    ks = (k_ref[0, :, h * A_QK:(h + 1) * A_QK].astype(F32) * (A_QK ** -0.5)).astype(BF16)
    dmat_t = jnp.where(g["seen"], src + b_r, -jnp.inf)
    inter = b_r + m_st
    m_row = jnp.maximum(jnp.max(dmat_t, axis=0, keepdims=True), inter)
    return dict(
        q_t=q_t, ks=ks, m_st=m_st, b_r=b_r, ic_r=ic_r, m_row=m_row,
        b_last=jnp.sum(g["logf_r"][f_idx:f_idx + 1, :], axis=1, keepdims=True),
        a_t=jnp.exp(dmat_t - m_row) * _dot(ks, q_t), w_inter=jnp.exp(inter - m_row))


def _mlstm_output(t, h, v_ref, o_ref, ct_ref, n_ref):
    v_t = v_ref[0, :, h * A_V:(h + 1) * A_V].astype(F32).T.astype(BF16)
    t["v_t"] = v_t
    num_t = (_dot(v_t, t["a_t"].astype(BF16))
             + t["w_inter"] * _dot(ct_ref[h].astype(BF16), t["q_t"]))
    qn = _dot(n_ref[h].astype(BF16), t["q_t"])[0:1, :]
    den = jnp.sum(t["a_t"], axis=0, keepdims=True) + t["w_inter"] * qn
    h_t = num_t / jnp.maximum(jnp.abs(den), jnp.exp(-t["m_row"]))
    o_ref[0, :, h * A_V:(h + 1) * A_V] = h_t.T


def _mlstm_state(t, h, ct_ref, n_ref, m_ref):
    g = t["b_last"] - t["b_r"] + t["ic_r"]
    m_new = jnp.maximum(t["b_last"] + t["m_st"], jnp.max(g, axis=1, keepdims=True))
    wk = jnp.exp(g - m_new)
    decay = jnp.exp(t["b_last"] + t["m_st"] - m_new)
    vw_t = (t["v_t"].astype(F32) * wk).astype(BF16)
    ct_ref[h] = decay * ct_ref[h] + _dot(vw_t, t["ks"])
    wk_rows = jnp.broadcast_to(wk, (SUBLANES, MLSTM_CHUNK)).astype(BF16)
    n_ref[h] = decay * n_ref[h] + _dot(wk_rows, t["ks"])
    m_ref[h] = jnp.broadcast_to(m_new, (SUBLANES, LANES))


def _mlstm_kernel(*refs):
    n_in, n_state = 5, 3
    ins = (refs[:n_in], refs[n_in:2 * n_in])
    outs = (refs[2 * n_in], refs[2 * n_in + 1])
    state = refs[2 * n_in + 2:]
    states = (state[:n_state], state[n_state:])

    @pl.when(pl.program_id(1) == 0)
    def _():
        for ref in state:
            ref[...] = jnp.zeros_like(ref)

    gates = [_mlstm_gates(ins[d][3], ins[d][4], reverse=bool(d)) for d in range(2)]
    chains = [(d, h) for h in range(A_HEADS) for d in range(2)]
    terms = {}
    for d, h in chains:
        q_ref, k_ref = ins[d][0], ins[d][1]
        terms[d, h] = _mlstm_scores(gates[d], h, bool(d), q_ref, k_ref, states[d][2])
    for d, h in chains:
        ct_ref, n_ref, _ = states[d]
        _mlstm_output(terms[d, h], h, ins[d][2], outs[d], ct_ref, n_ref)
    for d, h in chains:
        _mlstm_state(terms[d, h], h, *states[d])


def _mlstm(proj, gates, gates_t):
    bsz = proj.shape[0]
    n_lat_chunks = N_LAT // MLSTM_CHUNK

    def chunk_of(step, reverse):
        latent = (n_lat_chunks - step) if reverse else (step - 1)
        return jnp.where(step == 0, n_lat_chunks, latent)

    def specs(reverse):
        ck = functools.partial(chunk_of, reverse=reverse)
        return [
            pl.BlockSpec((1, MLSTM_CHUNK, A_HEADS * A_QK), lambda b, s: (b, ck(s), 0)),
            pl.BlockSpec((1, MLSTM_CHUNK, A_HEADS * A_QK), lambda b, s: (b, ck(s), 1)),
            pl.BlockSpec((1, MLSTM_CHUNK, D_MODEL), lambda b, s: (b, ck(s), COL_AV)),
            pl.BlockSpec((1, MLSTM_CHUNK, LANES), lambda b, s: (b, ck(s), 0)),
            pl.BlockSpec((1, N_GATE, MLSTM_CHUNK), lambda b, s: (b, 0, ck(s))),
        ]

    def out_spec(reverse):
        ck = functools.partial(chunk_of, reverse=reverse)
        return pl.BlockSpec((1, MLSTM_CHUNK, D_MODEL), lambda b, s: (b, ck(s), 0))

    state = [pltpu.VMEM((A_HEADS, A_V, A_QK), F32),
             pltpu.VMEM((A_HEADS, SUBLANES, A_QK), F32),
             pltpu.VMEM((A_HEADS, SUBLANES, LANES), F32)]
    args = (proj, proj, proj, gates, gates_t)
    return pl.pallas_call(
        _mlstm_kernel,
        grid=(bsz, N_TOK // MLSTM_CHUNK),
        in_specs=specs(False) + specs(True),
        out_specs=[out_spec(False), out_spec(True)],
        out_shape=[jax.ShapeDtypeStruct((bsz, N_TOK, D_MODEL), F32)] * 2,
        scratch_shapes=state + state,
        compiler_params=_cparams("parallel", "arbitrary"),
        name="mlstm_bidirectional",
    )(*args, *args)


def _chan_dft_kernel(u_ref, cs_ref, lat_ref, ctx_ref, p_ref, q_ref):
    t = pl.program_id(1)
    n_lane = C_GROUP_DIM // LANES

    def channel_dft(n_rows):
        for g in range(C_GROUPS):
            r = _dot(u_ref[0, :n_rows * GRID_W, g * C_GROUP_DIM:(g + 1) * C_GROUP_DIM], cs_ref[...])
            for i in range(n_rows):
                src = slice(i * GRID_W, (i + 1) * GRID_W)
                dst = slice(i * CH_PITCH, i * CH_PITCH + GRID_W)
                for j in range(n_lane):
                    p_ref[g * n_lane + j, dst, :] = r[src, j * LANES:(j + 1) * LANES]
                    q_ref[g * n_lane + j, dst, :] = r[src, C_GROUP_DIM + j * LANES:C_GROUP_DIM + (j + 1) * LANES]

    @pl.when(t < pl.num_programs(1) - 1)
    def _():
        channel_dft(CH_ROWS)
        for c in range(GRID_W):
            rows = pl.ds(c, CH_ROWS, stride=CH_PITCH)
            for j in range(D_MODEL // LANES):
                lanes = slice(c * D_MODEL + j * LANES, c * D_MODEL + (j + 1) * LANES)
                lat_ref[0, 0, :, lanes] = p_ref[j, rows, :].astype(BF16)
                lat_ref[0, 1, :, lanes] = q_ref[j, rows, :].astype(BF16)

    @pl.when(t == pl.num_programs(1) - 1)
    def _():
        n_rows = N_CTX // GRID_W
        channel_dft(n_rows)
        for i in range(n_rows):
            src = slice(i * CH_PITCH, i * CH_PITCH + GRID_W)
            dst = slice(i * GRID_W, (i + 1) * GRID_W)
            for j in range(D_MODEL // LANES):
                lanes = slice(j * LANES, (j + 1) * LANES)
                ctx_ref[0, 0, dst, lanes] = p_ref[j, src, :].astype(BF16)
                ctx_ref[0, 1, dst, lanes] = q_ref[j, src, :].astype(BF16)


def _chan_dft(proj, cs):
    bsz = proj.shape[0]
    tile = CH_ROWS * GRID_W
    n_lat = N_LAT // tile
    return pl.pallas_call(
        _chan_dft_kernel,
        grid=(bsz, n_lat + 1),
        in_specs=[pl.BlockSpec((1, tile, D_MODEL), lambda b, t: (b, t, COL_CU)),
                  pl.BlockSpec((C_GROUP_DIM, 2 * C_GROUP_DIM), lambda b, t: (0, 0))],
        out_specs=[pl.BlockSpec((1, 2, CH_ROWS, GRID_W * D_MODEL),
                                lambda b, t: (b, 0, jnp.minimum(t, n_lat - 1), 0)),
                   pl.BlockSpec((1, 2, N_CTX, D_MODEL), lambda b, t: (b, 0, 0, 0))],
        out_shape=[jax.ShapeDtypeStruct((bsz, 2, GRID_W, GRID_W * D_MODEL), BF16),
                   jax.ShapeDtypeStruct((bsz, 2, N_CTX, D_MODEL), BF16)],
        scratch_shapes=[pltpu.VMEM((D_MODEL // LANES, CH_ROWS * CH_PITCH, LANES), F32)] * 2,
        compiler_params=_cparams("parallel", "arbitrary"),
        name="fourier_channel_dft",
    )(proj, cs)


def _row_dft_kernel(pq_ref, cs_ref, tc_ref, ts_ref, z_ref):
    cos, sin = cs_ref[0], cs_ref[1]
    reps = D_MODEL // LANES
    for j in range(FFT_COLS):
        cols = slice(j * D_MODEL, (j + 1) * D_MODEL)
        p, q = pq_ref[0, 0, :, cols], pq_ref[0, 1, :, cols]
        a_re = _dot(cos, p) - _dot(sin, q)
        a_im = _dot(cos, q) + _dot(sin, p)
        tc = jnp.tile(tc_ref[j], (1, reps))
        ts = jnp.tile(ts_ref[j], (1, reps))
        z_ref[0, 0, :, cols] = (a_re * tc - a_im * ts).astype(z_ref.dtype)
        z_ref[0, 1, :, cols] = (a_re * ts + a_im * tc).astype(z_ref.dtype)


def _col_dft_kernel(z_ref, cs_ref, g_ref, y_in_ref, y_ref):
    del y_in_ref
    rows = FFT_K1 * GRID_W
    z_re = z_ref[0, 0].reshape(rows, D_MODEL)
    z_im = z_ref[0, 1].reshape(rows, D_MODEL)
    half = rows // 2
    for part in (slice(0, half), slice(half, rows)):
        y = _dot(cs_ref[0, part, :], z_re) - _dot(cs_ref[1, part, :], z_im)
        g = g_ref[0].reshape(rows, D_MODEL)[part].astype(F32)
        k2 = slice(part.start // FFT_K1, part.stop // FFT_K1)
        y_ref[0, k2] = (y * _silu(g)).astype(y_ref.dtype).reshape(half // FFT_K1, FFT_K1, D_MODEL)


def _seq_dft_latent(pq, proj, tables, y_prev):
    bsz = proj.shape[0]
    side = GRID_W
    cs_a, cs_b, tw_c, tw_s = tables
    n_rows_all = N_TOK // side
    full = lambda shape: pl.BlockSpec(shape, lambda b, t: (0,) * len(shape))
    z = pl.pallas_call(
        _row_dft_kernel,
        grid=(bsz, side // FFT_COLS),
        in_specs=[pl.BlockSpec((1, 2, side, FFT_COLS * D_MODEL), lambda b, t: (b, 0, 0, t)),
                  full((2, side, side)),
                  pl.BlockSpec((FFT_COLS, side, LANES), lambda b, t: (t, 0, 0)),
                  pl.BlockSpec((FFT_COLS, side, LANES), lambda b, t: (t, 0, 0))],
        out_specs=pl.BlockSpec((1, 2, side, FFT_COLS * D_MODEL), lambda b, t: (b, 0, 0, t)),
        out_shape=jax.ShapeDtypeStruct((bsz, 2, side, side * D_MODEL), BF16),
        compiler_params=_cparams("parallel", "parallel"),
        name="fourier_row_dft",
    )(pq, cs_a, tw_c, tw_s)
    y = pl.pallas_call(
        _col_dft_kernel,
        grid=(bsz, side // FFT_K1),
        in_specs=[pl.BlockSpec((1, 2, FFT_K1, side, D_MODEL), lambda b, t: (b, 0, t, 0, 0)),
                  full((2, FFT_K1 * side, FFT_K1 * side)),
                  pl.BlockSpec((1, side, FFT_K1, D_MODEL), lambda b, t: (b, 0, t, COL_CZ)),
                  pl.BlockSpec(memory_space=pl.ANY)],
        out_specs=pl.BlockSpec((1, side, FFT_K1, D_MODEL), lambda b, t: (b, 0, t, 0)),
        out_shape=jax.ShapeDtypeStruct((bsz, n_rows_all, side, D_MODEL), BF16),
        input_output_aliases={3: 0},
        compiler_params=_cparams("parallel", "parallel"),
        name="fourier_col_dft",
    )(z.reshape(bsz, 2, side, side, D_MODEL), cs_b, proj.reshape(bsz, n_rows_all, side, N_PROJ),
      y_prev.reshape(bsz, n_rows_all, side, D_MODEL))
    return y.reshape(bsz, N_TOK, D_MODEL)


def _seq_dft_ctx_kernel(a_ref, pq_ref, z_ref, y_in_ref, y_ref):
    del y_in_ref
    r = _dot(a_ref[:, :N_CTX], pq_ref[0, 0]) + _dot(a_ref[:, N_CTX:], pq_ref[0, 1])
    y_ref[0] = (r * _silu(z_ref[0].astype(F32))).astype(y_ref.dtype)


def _seq_dft_ctx(dft_ctx, pq, proj, y_c):
    bsz = proj.shape[0]
    ctx_blk = N_LAT // N_CTX
    return pl.pallas_call(
        _seq_dft_ctx_kernel,
        grid=(bsz,),
        in_specs=[
            pl.BlockSpec((N_CTX, 2 * N_CTX), lambda b: (0, 0)),
            pl.BlockSpec((1, 2, N_CTX, D_MODEL), lambda b: (b, 0, 0, 0)),
            pl.BlockSpec((1, N_CTX, D_MODEL), lambda b: (b, ctx_blk, COL_CZ)),
            pl.BlockSpec(memory_space=pl.ANY),
        ],
        out_specs=pl.BlockSpec((1, N_CTX, D_MODEL), lambda b: (b, ctx_blk, 0)),
        out_shape=jax.ShapeDtypeStruct(y_c.shape, y_c.dtype),
        input_output_aliases={3: 0},
        compiler_params=_cparams("parallel"),
        name="fourier_token_dft_ctx",
    )(dft_ctx, pq, proj, y_c)


def _merge_kernel(x_ref, hf_ref, hb_ref, ao_ref, az_ref, yb_ref, yc_ref, ga_ref, gb_ref, gc_ref,
                  mod_ref, nw_ref, wa_ref, wb_ref, wc_ref, wo_ref, o_ref, ya_ref, *, tile):
    b = pl.program_id(0)

    for h in range(A_HEADS):
        cols = slice(h * A_V, (h + 1) * A_V)
        s = hf_ref[0, :, cols] + hb_ref[0, :, cols]
        sn = s * lax.rsqrt(jnp.mean(s * s, axis=-1, keepdims=True) + EPS) * nw_ref[:, cols]
        gate = _sigmoid(ao_ref[0, :, cols].astype(F32)) * _silu(az_ref[0, :, cols].astype(F32))
        ya_ref[:, cols] = (sn * gate).astype(BF16)

    def branch(y, g_ref, w_ref):
        return _sigmoid(g_ref[0].astype(F32)) * _dot(y, w_ref[...])

    y = (branch(ya_ref[...], ga_ref, wa_ref) + branch(yb_ref[0], gb_ref, wb_ref)
         + branch(yc_ref[0], gc_ref, wc_ref))
    out = _dot(y.astype(BF16), wo_ref[...])
    gate_cols = slice(2 * D_MODEL, 3 * D_MODEL)
    gate = jnp.where(_ctx_rows(pl.program_id(1), tile),
                     mod_ref[pl.ds(B_CTX_ROW, 1), gate_cols], mod_ref[pl.ds(b, 1), gate_cols])
    o_ref[0] = x_ref[0] + gate * out


def _merge(xs, h_fwd, h_bwd, y_b, y_c, proj, mod_l, a_norm_w, wa, wb, wc, wo, n_rows, tile):
    bsz = xs.shape[0]
    col = lambda c: pl.BlockSpec((1, tile, D_MODEL), lambda b, t: (b, t, c))
    weight = pl.BlockSpec((D_MODEL, D_MODEL), lambda b, t: (0, 0))
    return pl.pallas_call(
        functools.partial(_merge_kernel, tile=tile),
        grid=(bsz, n_rows // tile),
        in_specs=[col(0), col(0), col(0), col(COL_AO), col(COL_AZ), col(0), col(0),
                  col(COL_MG), col(COL_MG + 1), col(COL_MG + 2),
                  pl.BlockSpec((SUBLANES, 3 * D_MODEL), lambda b, t: (0, 0)),
                  pl.BlockSpec((1, D_MODEL), lambda b, t: (0, 0)),
                  weight, weight, weight, weight],
        out_specs=col(0),
        out_shape=jax.ShapeDtypeStruct((bsz, n_rows, D_MODEL), F32),
        scratch_shapes=[pltpu.VMEM((tile, D_MODEL), BF16)],
        compiler_params=_cparams("parallel", "parallel"),
        name="merge_out_residual",
    )(xs, h_fwd, h_bwd, proj, proj, y_b, y_c, proj, proj, proj, mod_l, a_norm_w, wa, wb, wc, wo)


def _rope_tables():
    n = jnp.arange(N_LAT, dtype=jnp.int32)
    row = (n // GRID_W).astype(F32)
    col = (n % GRID_W).astype(F32)
    inv_freq = ROPE_THETA ** (-jnp.arange(0, ROPE_AXIS_DIM, 2, dtype=F32) / ROPE_AXIS_DIM)
    ang_r = row[:, None] * inv_freq
    ang_c = col[:, None] * inv_freq
    cos = jnp.concatenate([jnp.cos(ang_r)] * 2 + [jnp.cos(ang_c)] * 2, axis=-1)
    sin = jnp.concatenate([-jnp.sin(ang_r), jnp.sin(ang_r), -jnp.sin(ang_c), jnp.sin(ang_c)], axis=-1)
    cos = jnp.concatenate([cos, jnp.ones((N_CTX, B_QK), F32)], axis=0)
    sin = jnp.concatenate([sin, jnp.zeros((N_CTX, B_QK), F32)], axis=0)
    reps = LANES // B_QK
    return jnp.tile(cos, (1, reps)), jnp.tile(sin, (1, reps))


def _dft_cos_sin(n, scale):
    idx = jnp.arange(n, dtype=jnp.int32)
    ang = ((idx[:, None] * idx[None, :]) % n).astype(F32) * (2.0 * math.pi / n)
    return jnp.cos(ang) * scale, jnp.sin(ang) * scale


def _dft_tables():
    c_ch, s_ch = _dft_cos_sin(C_GROUP_DIM, 1.0)
    chan = jnp.concatenate([c_ch, s_ch], axis=1).astype(BF16)
    c_c, s_c = _dft_cos_sin(N_CTX, (N_CTX * C_GROUP_DIM) ** -0.5)
    ctx = jnp.concatenate([c_c, -s_c], axis=1).astype(BF16)
    c_s, s_s = _dft_cos_sin(GRID_W, 1.0)
    stage_a = jnp.stack([c_s, s_s]).astype(BF16)
    eye = jnp.eye(FFT_K1, dtype=F32)
    expand = lambda t: jnp.einsum("kn,ij->kijn", t, eye).reshape(
        FFT_K1 * GRID_W, FFT_K1 * GRID_W)
    stage_b = (jnp.stack([expand(c_s), expand(s_s)]) * (N_LAT * C_GROUP_DIM) ** -0.5).astype(BF16)
    idx = jnp.arange(GRID_W, dtype=jnp.int32)
    ang = (idx[:, None] * idx[None, :]).astype(F32) * (2.0 * math.pi / N_LAT)
    lanes = lambda t: jnp.broadcast_to(t[:, :, None], (GRID_W, GRID_W, LANES))
    return chan, ctx, (stage_a, stage_b, lanes(jnp.cos(ang)), lanes(jnp.sin(ang)))


def _group_mean_matrix():
    g = np.arange(LANES) // B_QK
    return jnp.asarray((g[:, None] == g[None, :]) / B_QK, dtype=BF16)


def kernel(x, c, ctx, c_ctx, norm_w, w_ada, b_ada, w_in, b_if, a_norm_w, q_norm_w, k_norm_w,
           lambda_q1, lambda_k1, lambda_q2, lambda_k2, subln_w, w_a_out, w_b_out, w_c_out, w_out):
    bsz = x.shape[0]
    xs = jnp.concatenate([x, ctx], axis=1)
    c_rows = jnp.concatenate(
        [c, c_ctx[None, :], jnp.zeros((SUBLANES - bsz - 1, D_MODEL), F32)], axis=0)
    mod, lam = _modulation(c_rows, w_ada, b_ada, lambda_q1, lambda_k1, lambda_q2, lambda_k2)

    cos_t, sin_t = _rope_tables()
    dft_chan, dft_ctx, dft_lat = _dft_tables()
    group_mean = _group_mean_matrix()

    w_in_t = jnp.swapaxes(w_in, 1, 2)
    w_tail_t = w_in_t[:, w_in_t.shape[1] - N_GATE:, :]
    lane_pad = LANES - N_GATE
    pad_lanes = lambda w: jnp.pad(w, ((0, 0),) * (w.ndim - 1) + ((0, lane_pad),))
    w_gate = pad_lanes(w_in[:, :, GATE_COL0:GATE_COL0 + N_GATE])
    b_gate = pad_lanes(b_if)
    wa, wb, wc, wo = (w.astype(BF16) for w in (w_a_out, w_b_out, w_c_out, w_out))
    tile2 = lambda w: jnp.tile(w[None, :], (1, LANES // B_QK))

    for l in range(DEPTH):
        last = l == DEPTH - 1
        h, gates = _norm_mod(xs, norm_w[l][None, :], mod[l], w_gate[l], b_gate[l][None, :])
        proj = _in_projection(h.reshape(bsz * N_TOK, D_MODEL), w_in_t, w_tail_t, l).reshape(bsz, N_TOK, N_PROJ)

        gates_t = jnp.swapaxes(gates[:, :, :N_GATE], 1, 2)
        h_fwd, h_bwd = _mlstm(proj, gates, gates_t)

        qh, kt = _qk_prep(proj, cos_t, sin_t, tile2(q_norm_w[l]), tile2(k_norm_w[l]), group_mean)
        lam_l = lam[l, 0, :1]
        lam_init = 0.8 - 0.6 * math.exp(-0.3 * l)
        y_b = _diff_attention(lam_l, qh, kt, proj, subln_w[l][None, :], lam_init, h, ctx=False)
        if not last:
            y_b = _diff_attention(lam_l, qh, kt, proj, subln_w[l][None, :], lam_init, y_b, ctx=True)

        pq, pq_ctx = _chan_dft(proj, dft_chan)
        y_c = _seq_dft_latent(pq, proj, dft_lat, qh)

        if last:
            return _merge(xs, h_fwd, h_bwd, y_b, y_c, proj, mod[l], a_norm_w[l][None, :],
                          wa[l], wb[l], wc[l], wo[l], n_rows=N_LAT, tile=LAST_TILE)
        y_c = _seq_dft_ctx(dft_ctx, pq_ctx, proj, y_c)
        xs = _merge(xs, h_fwd, h_bwd, y_b, y_c, proj, mod[l], a_norm_w[l][None, :],
                    wa[l], wb[l], wc[l], wo[l], n_rows=N_TOK, tile=MERGE_TILE)
```

```python
import functools
import math

import jax
import jax.numpy as jnp
import numpy as np
from jax import lax
from jax.experimental import pallas as pl
from jax.experimental.pallas import tpu as pltpu

F32 = jnp.float32
BF16 = jnp.bfloat16
LOG2E = math.log2(math.e)

D_MODEL = 1024
DEPTH = 4
N_LAT = 4096
N_CTX = 256
N_TOK = N_LAT + N_CTX
GRID_W = 64
EPS = 1e-6

A_HEADS, A_QK, A_V = 4, 128, 256
B_HEADS, B_QK, B_V = 8, 64, 128
C_GROUPS, C_GROUP_DIM = 4, 256
ROPE_AXIS_DIM = B_QK // 2
ROPE_THETA = 10000.0
N_GATE = 4 * A_HEADS
BATCH = 4
B_CTX_ROW = BATCH

COL_AQK, COL_AV, COL_AO, COL_AZ, COL_BQ, COL_BK, COL_BV, COL_BZ, COL_CU, COL_CZ, COL_MG = (
    0, 1, 2, 3, 4, 5, 6, 7, 8, 9, 10)
N_PROJ = 13 * D_MODEL
GATE_COL0 = 2 * A_HEADS * A_QK + A_HEADS * A_V

LANES = 128
SUBLANES = 8
VMEM_LIMIT_BYTES = 56 * 1024 * 1024

TOK_TILE = 544
MM_TM = 2176
MM_TN = 1024
ATT_TQ = 256
ATT_ROWS = 2 * ATT_TQ
QK_TILE = 256
MERGE_TILE = 272
LAST_TILE = 256
MLSTM_CHUNK = 256
CH_ROWS = 16
CH_PITCH = GRID_W + 8
FFT_COLS = 8
FFT_K1 = 16


def _cparams(*sem):
    return pltpu.CompilerParams(dimension_semantics=sem, vmem_limit_bytes=VMEM_LIMIT_BYTES)


def _split3(x):
    hi = x.astype(BF16)
    r1 = x - hi.astype(F32)
    mid = r1.astype(BF16)
    lo = (r1 - mid.astype(F32)).astype(BF16)
    return hi, mid, lo


def _dot(a, b):
    return jnp.dot(a, b, preferred_element_type=F32)


def _dot_f32_rhs(a_exact, b):
    hi, mid, lo = _split3(b)
    return _dot(a_exact, hi) + _dot(a_exact, mid) + _dot(a_exact, lo)


def _dot_f32_lhs(a, b_exact):
    hi, mid, lo = _split3(a)
    return _dot(hi, b_exact) + _dot(mid, b_exact) + _dot(lo, b_exact)


def _dot_2pass_lhs(a, b_exact):
    hi = a.astype(BF16)
    mid = (a - hi.astype(F32)).astype(BF16)
    return _dot(hi, b_exact) + _dot(mid, b_exact)


def _dot_f32(a, b):
    a_hi = a.astype(BF16)
    a_lo = (a - a_hi.astype(F32)).astype(BF16)
    b_hi = b.astype(BF16)
    b_lo = (b - b_hi.astype(F32)).astype(BF16)
    return _dot(a_hi, b_hi) + _dot(a_lo, b_hi) + _dot(a_hi, b_lo)


def _sigmoid(x):
    return 0.5 * jnp.tanh(0.5 * x) + 0.5


def _silu(x):
    return x * _sigmoid(x)


def _log_sigmoid(x):
    return jnp.minimum(x, 0.0) - jnp.log(1.0 + jnp.exp(-jnp.abs(x)))


def _mod_kernel(c_ref, w_ref, b_ref, q1_ref, k1_ref, q2_ref, k2_ref, mod_ref, lam_ref):
    sc = _silu(c_ref[...])
    mod_ref[0] = _dot_f32(sc, w_ref[0]) + b_ref[0]
    layer = jnp.full((1, LANES), pl.program_id(0), dtype=jnp.int32).astype(F32)
    lam_init = 0.8 - 0.6 * jnp.exp(-0.3 * layer)
    s1 = jnp.sum(q1_ref[0] * k1_ref[0], axis=-1, keepdims=True)
    s2 = jnp.sum(q2_ref[0] * k2_ref[0], axis=-1, keepdims=True)
    lam_ref[0] = jnp.exp(s1) - jnp.exp(s2) + lam_init


def _modulation(c_rows, w_ada, b_ada, lq1, lk1, lq2, lk2):
    n_col = 3
    vec = lambda a: a.reshape(DEPTH, 1, B_QK)
    lam_spec = pl.BlockSpec((1, 1, B_QK), lambda l, j: (l, 0, 0))
    return pl.pallas_call(
        _mod_kernel,
        grid=(DEPTH, n_col),
        in_specs=[
            pl.BlockSpec((SUBLANES, D_MODEL), lambda l, j: (0, 0)),
            pl.BlockSpec((1, D_MODEL, D_MODEL), lambda l, j: (l, 0, j)),
            pl.BlockSpec((1, 1, D_MODEL), lambda l, j: (l, 0, j)),
            lam_spec, lam_spec, lam_spec, lam_spec,
        ],
        out_specs=[
            pl.BlockSpec((1, SUBLANES, D_MODEL), lambda l, j: (l, 0, j)),
            pl.BlockSpec((1, 1, LANES), lambda l, j: (l, 0, 0)),
        ],
        out_shape=[
            jax.ShapeDtypeStruct((DEPTH, SUBLANES, 3 * D_MODEL), F32),
            jax.ShapeDtypeStruct((DEPTH, 1, LANES), F32),
        ],
        compiler_params=_cparams("arbitrary", "arbitrary"),
        name="adaln_modulation",
    )(c_rows, w_ada, b_ada.reshape(DEPTH, 1, 3 * D_MODEL), vec(lq1), vec(lk1), vec(lq2), vec(lk2))


def _ctx_rows(tile_idx, tile):
    row = tile_idx * tile + lax.broadcasted_iota(jnp.int32, (tile, 1), 0)
    return row >= N_LAT


def _norm_mod_kernel(x_ref, nw_ref, mod_ref, wg_ref, bif_ref, h_ref, g_ref):
    b = pl.program_id(0)
    x = x_ref[0]
    y = x * lax.rsqrt(jnp.mean(x * x, axis=-1, keepdims=True) + EPS) * nw_ref[...]
    is_ctx = _ctx_rows(pl.program_id(1), TOK_TILE)
    m_lat = mod_ref[pl.ds(b, 1), :]
    m_ctx = mod_ref[pl.ds(B_CTX_ROW, 1), :]
    shift = jnp.where(is_ctx, m_ctx[:, 0:D_MODEL], m_lat[:, 0:D_MODEL])
    scale = jnp.where(is_ctx, m_ctx[:, D_MODEL:2 * D_MODEL], m_lat[:, D_MODEL:2 * D_MODEL])
    h = y * (1.0 + scale) + shift
    h_ref[0] = h.astype(BF16)
    g_ref[0] = _dot_f32(h, wg_ref[...]) + bif_ref[...]


def _norm_mod(xs, norm_w, mod_l, w_gate, b_if):
    bsz = xs.shape[0]
    tok = pl.BlockSpec((1, TOK_TILE, D_MODEL), lambda b, t: (b, t, 0))
    full = lambda shape: pl.BlockSpec(shape, lambda b, t: (0,) * len(shape))
    return pl.pallas_call(
        _norm_mod_kernel,
        grid=(bsz, N_TOK // TOK_TILE),
        in_specs=[tok, full((1, D_MODEL)), full((SUBLANES, 3 * D_MODEL)),
                  full((D_MODEL, LANES)), full((1, LANES))],
        out_specs=[tok, pl.BlockSpec((1, TOK_TILE, LANES), lambda b, t: (b, t, 0))],
        out_shape=[jax.ShapeDtypeStruct((bsz, N_TOK, D_MODEL), BF16),
                   jax.ShapeDtypeStruct((bsz, N_TOK, LANES), F32)],
        compiler_params=_cparams("parallel", "parallel"),
        name="norm_modulate",
    )(xs, norm_w, mod_l, w_gate, b_if)


def _in_proj_kernel(a_ref, w0_ref, w1_ref, tail_ref, o_ref, wb_ref):
    j = pl.program_id(0)
    n_plain = GATE_COL0 // MM_TN

    @pl.when((pl.program_id(1) == 0) & (j < n_plain))
    def _():
        wb_ref[...] = w0_ref[0].astype(BF16)

    @pl.when((pl.program_id(1) == 0) & (j >= n_plain))
    def _():
        nxt = jnp.where(j == pl.num_programs(0) - 1, tail_ref[0], w1_ref[0, :N_GATE, :])
        wb_ref[:MM_TN - N_GATE, :] = w0_ref[0, N_GATE:, :].astype(BF16)
        wb_ref[MM_TN - N_GATE:, :] = nxt.astype(BF16)

    o_ref[...] = lax.dot_general(a_ref[...], wb_ref[...], (((1,), (1,)), ((), ())),
                                 preferred_element_type=F32).astype(o_ref.dtype)


def _in_projection(h2d, w_in_t, w_tail_t, layer):
    m, k = h2d.shape
    n_col = N_PROJ // MM_TN
    return pl.pallas_call(
        _in_proj_kernel,
        grid=(n_col, m // MM_TM),
        in_specs=[pl.BlockSpec((MM_TM, k), lambda j, i: (i, 0)),
                  pl.BlockSpec((1, MM_TN, k), lambda j, i: (layer, j, 0)),
                  pl.BlockSpec((1, MM_TN, k), lambda j, i: (layer, jnp.minimum(j + 1, n_col - 1), 0)),
                  pl.BlockSpec((1, N_GATE, k), lambda j, i: (layer, 0, 0))],
        out_specs=pl.BlockSpec((MM_TM, MM_TN), lambda j, i: (i, j)),
        out_shape=jax.ShapeDtypeStruct((m, N_PROJ), BF16),
        scratch_shapes=[pltpu.VMEM((MM_TN, k), BF16)],
        compiler_params=_cparams("arbitrary", "arbitrary"),
        name="in_projection",
    )(h2d, w_in_t, w_in_t, w_tail_t)


def _qk_prep_kernel(q_ref, k_ref, cos_ref, sin_ref, qw_ref, kw_ref, g_ref, qo_ref, kt_ref):
    cos = cos_ref[...]
    sin = sin_ref[...]
    group_mean = g_ref[...]
    lane = lax.broadcasted_iota(jnp.int32, (1, LANES), 1)
    first_half = (lane % ROPE_AXIS_DIM) < (ROPE_AXIS_DIM // 2)

    def prep(x_bf16, w):
        x = x_bf16.astype(F32)
        ms = _dot_2pass_lhs(x * x, group_mean)
        y = x * lax.rsqrt(ms + EPS) * w
        half = ROPE_AXIS_DIM // 2
        partner = jnp.where(first_half, pltpu.roll(y, LANES - half, 1), pltpu.roll(y, half, 1))
        return y * cos + partner * sin

    for s in range(D_MODEL // LANES):
        cols = slice(s * LANES, (s + 1) * LANES)
        qo_ref[0, :, cols] = (prep(q_ref[0, :, cols], qw_ref[...]) * (B_QK ** -0.5 * LOG2E)).astype(BF16)
        kt_ref[0, cols, :] = prep(k_ref[0, :, cols], kw_ref[...]).T.astype(BF16)


def _qk_prep(proj, cos_t, sin_t, qw, kw, group_mean):
    bsz = proj.shape[0]
    col = lambda c: pl.BlockSpec((1, QK_TILE, D_MODEL), lambda b, t: (b, t, c))
    tab = pl.BlockSpec((QK_TILE, LANES), lambda b, t: (t, 0))
    full = lambda shape: pl.BlockSpec(shape, lambda b, t: (0,) * len(shape))
    return pl.pallas_call(
        _qk_prep_kernel,
        grid=(bsz, N_TOK // QK_TILE),
        in_specs=[col(COL_BQ), col(COL_BK), tab, tab, full((1, LANES)), full((1, LANES)),
                  full((LANES, LANES))],
        out_specs=[pl.BlockSpec((1, QK_TILE, D_MODEL), lambda b, t: (b, t, 0)),
                   pl.BlockSpec((1, D_MODEL, QK_TILE), lambda b, t: (b, 0, t))],
        out_shape=[jax.ShapeDtypeStruct((bsz, N_TOK, D_MODEL), BF16),
                   jax.ShapeDtypeStruct((bsz, D_MODEL, N_TOK), BF16)],
        compiler_params=_cparams("parallel", "parallel"),
        name="qk_norm_rope",
    )(proj, proj, cos_t, sin_t, qw, kw, group_mean)


def _diff_attn_body(lam_ref, q_ref, kt_ref, v_ref, z_ref, sw_ref, o_ref, vp_ref, lam_init):
    @pl.when(pl.program_id(2) == 0)
    def _():
        vp_ref[:, :B_V] = v_ref[0]
        vp_ref[:, B_V:] = jnp.ones((vp_ref.shape[0], B_V), BF16)

    chains = [slice(c * ATT_TQ, (c + 1) * ATT_TQ) for c in range(q_ref.shape[1] // ATT_TQ)]
    lane = lax.broadcasted_iota(jnp.int32, (ATT_TQ, B_V), 1)
    zero = jnp.zeros((ATT_TQ, B_V), BF16)

    def scores(rows, first_map):
        q = q_ref[0, rows, :]
        return _dot(jnp.where((lane < B_QK) if first_map else (lane >= B_QK), q, zero), kt_ref[0])

    def numerators(s):
        return jnp.exp2(s - jnp.max(s, axis=-1, keepdims=True)).astype(BF16)

    s0 = [scores(rows, True) for rows in chains]
    s1 = [scores(rows, False) for rows in chains]
    u0 = [_dot(numerators(s), vp_ref[...]) for s in s0]
    u1 = [_dot(numerators(s), vp_ref[...]) for s in s1]
    for rows, a, b in zip(chains, u0, u1):
        o = a[:, :B_V] / a[:, B_V:] - lam_ref[0] * (b[:, :B_V] / b[:, B_V:])
        y = o * lax.rsqrt(jnp.mean(o * o, axis=-1, keepdims=True) + EPS) * sw_ref[...]
        o_ref[0, rows, :] = (y * (1.0 - lam_init)
                             * _silu(z_ref[0, rows, :].astype(F32))).astype(o_ref.dtype)


def _diff_attn_kernel(lam_ref, q_ref, kt_ref, v_ref, z_ref, sw_ref, y_in_ref, o_ref, vp_ref,
                      *, lam_init):
    del y_in_ref
    _diff_attn_body(lam_ref, q_ref, kt_ref, v_ref, z_ref, sw_ref, o_ref, vp_ref, lam_init)


def _diff_attention(lam, qh, kt, proj, subln_w, lam_init, y_prev, ctx):
    bsz = qh.shape[0]
    heads_per_block = D_MODEL // B_V
    n_keys = N_CTX if ctx else N_TOK
    rows = N_CTX if ctx else ATT_ROWS
    q0 = N_LAT // rows if ctx else 0
    k0 = N_LAT // N_CTX if ctx else 0
    tile = lambda c0: pl.BlockSpec((1, rows, B_V), lambda b, h, i: (b, q0 + i, c0 + h))
    in_specs = [pl.BlockSpec(memory_space=pltpu.SMEM),
                tile(0),
                pl.BlockSpec((1, B_V, n_keys), lambda b, h, i: (b, h, k0)),
                pl.BlockSpec((1, n_keys, B_V), lambda b, h, i: (b, k0, COL_BV * heads_per_block + h)),
                tile(COL_BZ * heads_per_block),
                pl.BlockSpec((1, B_V), lambda b, h, i: (0, 0)),
                pl.BlockSpec(memory_space=pl.ANY)]
    return pl.pallas_call(
        functools.partial(_diff_attn_kernel, lam_init=lam_init),
        grid=(bsz, B_HEADS, (N_CTX if ctx else N_LAT) // rows),
        in_specs=in_specs,
        out_specs=tile(0),
        out_shape=jax.ShapeDtypeStruct((bsz, N_TOK, D_MODEL), BF16),
        scratch_shapes=[pltpu.VMEM((n_keys, 2 * B_V), BF16)],
        input_output_aliases={6: 0},
        compiler_params=_cparams("parallel", "parallel", "arbitrary"),
        name="diff_attention_ctx" if ctx else "diff_attention",
    )(lam, qh, kt, proj, proj, subln_w, y_prev)


def _mlstm_gates(gc_ref, gr_ref, reverse):
    chunk = MLSTM_CHUNK
    row = lax.broadcasted_iota(jnp.int32, (chunk, chunk), 0)
    col = lax.broadcasted_iota(jnp.int32, (chunk, chunk), 1)
    seen = (row >= col) if reverse else (row <= col)
    vis_ts = jnp.where((row <= col) if reverse else (row >= col), 1.0, 0.0).astype(BF16)
    vis_st = jnp.where(seen, 1.0, 0.0).astype(BF16)
    gate_c = gc_ref[0]
    gate_r = gr_ref[0]
    logf_r = _log_sigmoid(gate_r)
    return dict(seen=seen, gate_c=gate_c, gate_r=gate_r, logf_r=logf_r,
                cum_c=_dot_f32_rhs(vis_ts, _log_sigmoid(gate_c)),
                cum_r=_dot_f32_lhs(logf_r, vis_st))


def _mlstm_scores(g, h, reverse, q_ref, k_ref, m_ref):
    d = 1 if reverse else 0
    i_idx = (2 * d) * A_HEADS + h
    f_idx = (2 * d + 1) * A_HEADS + h
    src = g["gate_c"][:, i_idx:i_idx + 1] - g["cum_c"][:, f_idx:f_idx + 1]
    ic_r = g["gate_r"][i_idx:i_idx + 1, :]
    b_r = g["cum_r"][f_idx:f_idx + 1, :]
    m_st = m_ref[h, 0:1, 0:1]
    q_t = q_ref[0, :, h * A_QK:(h + 1) * A_QK].astype(F32).T.astype(BF16)
    ks = (k_ref[0, :, h * A_QK:(h + 1) * A_QK].astype(F32) * (A_QK ** -0.5)).astype(BF16)
    dmat_t = jnp.where(g["seen"], src + b_r, -jnp.inf)
    inter = b_r + m_st
    m_row = jnp.maximum(jnp.max(dmat_t, axis=0, keepdims=True), inter)
    return dict(
        q_t=q_t, ks=ks, m_st=m_st, b_r=b_r, ic_r=ic_r, m_row=m_row,
        b_last=jnp.sum(g["logf_r"][f_idx:f_idx + 1, :], axis=1, keepdims=True),
        a_t=jnp.exp(dmat_t - m_row) * _dot(ks, q_t), w_inter=jnp.exp(inter - m_row))


def _mlstm_output(t, h, v_ref, o_ref, ct_ref, n_ref):
    v_t = v_ref[0, :, h * A_V:(h + 1) * A_V].astype(F32).T.astype(BF16)
    t["v_t"] = v_t
    num_t = (_dot(v_t, t["a_t"].astype(BF16))
             + t["w_inter"] * _dot(ct_ref[h].astype(BF16), t["q_t"]))
    qn = _dot(n_ref[h].astype(BF16), t["q_t"])[0:1, :]
    den = jnp.sum(t["a_t"], axis=0, keepdims=True) + t["w_inter"] * qn
    h_t = num_t / jnp.maximum(jnp.abs(den), jnp.exp(-t["m_row"]))
    o_ref[0, :, h * A_V:(h + 1) * A_V] = h_t.T.astype(o_ref.dtype)


def _mlstm_state(t, h, ct_ref, n_ref, m_ref):
    g = t["b_last"] - t["b_r"] + t["ic_r"]
    m_new = jnp.maximum(t["b_last"] + t["m_st"], jnp.max(g, axis=1, keepdims=True))
    wk = jnp.exp(g - m_new)
    decay = jnp.exp(t["b_last"] + t["m_st"] - m_new)
    vw_t = (t["v_t"].astype(F32) * wk).astype(BF16)
    ct_ref[h] = decay * ct_ref[h] + _dot(vw_t, t["ks"])
    wk_rows = jnp.broadcast_to(wk, (SUBLANES, MLSTM_CHUNK)).astype(BF16)
    n_ref[h] = decay * n_ref[h] + _dot(wk_rows, t["ks"])
    m_ref[h] = jnp.broadcast_to(m_new, (SUBLANES, LANES))


def _mlstm_kernel(*refs):
    n_in, n_state = 5, 3
    ins = (refs[:n_in], refs[n_in:2 * n_in])
    outs = (refs[2 * n_in], refs[2 * n_in + 1])
    state = refs[2 * n_in + 2:]
    states = (state[:n_state], state[n_state:])

    @pl.when(pl.program_id(1) == 0)
    def _():
        for ref in state:
            ref[...] = jnp.zeros_like(ref)

    gates = [_mlstm_gates(ins[d][3], ins[d][4], reverse=bool(d)) for d in range(2)]
    chains = [(d, h) for h in range(A_HEADS) for d in range(2)]
    terms = {}
    for d, h in chains:
        q_ref, k_ref = ins[d][0], ins[d][1]
        terms[d, h] = _mlstm_scores(gates[d], h, bool(d), q_ref, k_ref, states[d][2])
    for d, h in chains:
        ct_ref, n_ref, _ = states[d]
        _mlstm_output(terms[d, h], h, ins[d][2], outs[d], ct_ref, n_ref)
    for d, h in chains:
        _mlstm_state(terms[d, h], h, *states[d])


def _mlstm(proj, gates, gates_t):
    bsz = proj.shape[0]
    n_lat_chunks = N_LAT // MLSTM_CHUNK

    def chunk_of(step, reverse):
        latent = (n_lat_chunks - step) if reverse else (step - 1)
        return jnp.where(step == 0, n_lat_chunks, latent)

    def specs(reverse):
        ck = functools.partial(chunk_of, reverse=reverse)
        return [
            pl.BlockSpec((1, MLSTM_CHUNK, A_HEADS * A_QK), lambda b, s: (b, ck(s), 0)),
            pl.BlockSpec((1, MLSTM_CHUNK, A_HEADS * A_QK), lambda b, s: (b, ck(s), 1)),
            pl.BlockSpec((1, MLSTM_CHUNK, D_MODEL), lambda b, s: (b, ck(s), COL_AV)),
            pl.BlockSpec((1, MLSTM_CHUNK, LANES), lambda b, s: (b, ck(s), 0)),
            pl.BlockSpec((1, N_GATE, MLSTM_CHUNK), lambda b, s: (b, 0, ck(s))),
        ]

    def out_spec(reverse):
        ck = functools.partial(chunk_of, reverse=reverse)
        return pl.BlockSpec((1, MLSTM_CHUNK, D_MODEL), lambda b, s: (b, ck(s), 0))

    state = [pltpu.VMEM((A_HEADS, A_V, A_QK), F32),
             pltpu.VMEM((A_HEADS, SUBLANES, A_QK), F32),
             pltpu.VMEM((A_HEADS, SUBLANES, LANES), F32)]
    args = (proj, proj, proj, gates, gates_t)
    return pl.pallas_call(
        _mlstm_kernel,
        grid=(bsz, N_TOK // MLSTM_CHUNK),
        in_specs=specs(False) + specs(True),
        out_specs=[out_spec(False), out_spec(True)],
        out_shape=[jax.ShapeDtypeStruct((bsz, N_TOK, D_MODEL), BF16)] * 2,
        scratch_shapes=state + state,
        compiler_params=_cparams("parallel", "arbitrary"),
        name="mlstm_bidirectional",
    )(*args, *args)


def _chan_dft_kernel(u_ref, cs_ref, lat_ref, ctx_ref, p_ref, q_ref):
    t = pl.program_id(1)
    n_lane = C_GROUP_DIM // LANES

    def channel_dft(n_rows):
        for g in range(C_GROUPS):
            r = _dot(u_ref[0, :n_rows * GRID_W, g * C_GROUP_DIM:(g + 1) * C_GROUP_DIM], cs_ref[...])
            for i in range(n_rows):
                src = slice(i * GRID_W, (i + 1) * GRID_W)
                dst = slice(i * CH_PITCH, i * CH_PITCH + GRID_W)
                for j in range(n_lane):
                    p_ref[g * n_lane + j, dst, :] = r[src, j * LANES:(j + 1) * LANES]
                    q_ref[g * n_lane + j, dst, :] = r[src, C_GROUP_DIM + j * LANES:C_GROUP_DIM + (j + 1) * LANES]

    @pl.when(t < pl.num_programs(1) - 1)
    def _():
        channel_dft(CH_ROWS)
        for c in range(GRID_W):
            rows = pl.ds(c, CH_ROWS, stride=CH_PITCH)
            for j in range(D_MODEL // LANES):
                lanes = slice(c * D_MODEL + j * LANES, c * D_MODEL + (j + 1) * LANES)
                lat_ref[0, 0, :, lanes] = p_ref[j, rows, :].astype(BF16)
                lat_ref[0, 1, :, lanes] = q_ref[j, rows, :].astype(BF16)

    @pl.when(t == pl.num_programs(1) - 1)
    def _():
        n_rows = N_CTX // GRID_W
        channel_dft(n_rows)
        for i in range(n_rows):
            src = slice(i * CH_PITCH, i * CH_PITCH + GRID_W)
            dst = slice(i * GRID_W, (i + 1) * GRID_W)
            for j in range(D_MODEL // LANES):
                lanes = slice(j * LANES, (j + 1) * LANES)
                ctx_ref[0, 0, dst, lanes] = p_ref[j, src, :].astype(BF16)
                ctx_ref[0, 1, dst, lanes] = q_ref[j, src, :].astype(BF16)


def _chan_dft(proj, cs):
    bsz = proj.shape[0]
    tile = CH_ROWS * GRID_W
    n_lat = N_LAT // tile
    return pl.pallas_call(
        _chan_dft_kernel,
        grid=(bsz, n_lat + 1),
        in_specs=[pl.BlockSpec((1, tile, D_MODEL), lambda b, t: (b, t, COL_CU)),
                  pl.BlockSpec((C_GROUP_DIM, 2 * C_GROUP_DIM), lambda b, t: (0, 0))],
        out_specs=[pl.BlockSpec((1, 2, CH_ROWS, GRID_W * D_MODEL),
                                lambda b, t: (b, 0, jnp.minimum(t, n_lat - 1), 0)),
                   pl.BlockSpec((1, 2, N_CTX, D_MODEL), lambda b, t: (b, 0, 0, 0))],
        out_shape=[jax.ShapeDtypeStruct((bsz, 2, GRID_W, GRID_W * D_MODEL), BF16),
                   jax.ShapeDtypeStruct((bsz, 2, N_CTX, D_MODEL), BF16)],
        scratch_shapes=[pltpu.VMEM((D_MODEL // LANES, CH_ROWS * CH_PITCH, LANES), F32)] * 2,
        compiler_params=_cparams("parallel", "arbitrary"),
        name="fourier_channel_dft",
    )(proj, cs)


def _row_dft_kernel(pq_ref, cs_ref, tc_ref, ts_ref, z_ref):
    cos, sin = cs_ref[0], cs_ref[1]
    reps = D_MODEL // LANES
    for j in range(FFT_COLS):
        cols = slice(j * D_MODEL, (j + 1) * D_MODEL)
        p, q = pq_ref[0, 0, :, cols], pq_ref[0, 1, :, cols]
        a_re = _dot(cos, p) - _dot(sin, q)
        a_im = _dot(cos, q) + _dot(sin, p)
        tc = jnp.tile(tc_ref[j], (1, reps))
        ts = jnp.tile(ts_ref[j], (1, reps))
        z_ref[0, 0, :, cols] = (a_re * tc - a_im * ts).astype(z_ref.dtype)
        z_ref[0, 1, :, cols] = (a_re * ts + a_im * tc).astype(z_ref.dtype)


def _col_dft_kernel(z_ref, cs_ref, g_ref, y_in_ref, y_ref):
    del y_in_ref
    rows = FFT_K1 * GRID_W
    z_re = z_ref[0, 0].reshape(rows, D_MODEL)
    z_im = z_ref[0, 1].reshape(rows, D_MODEL)
    half = rows // 2
    for part in (slice(0, half), slice(half, rows)):
        y = _dot(cs_ref[0, part, :], z_re) - _dot(cs_ref[1, part, :], z_im)
        g = g_ref[0].reshape(rows, D_MODEL)[part].astype(F32)
        k2 = slice(part.start // FFT_K1, part.stop // FFT_K1)
        y_ref[0, k2] = (y * _silu(g)).astype(y_ref.dtype).reshape(half // FFT_K1, FFT_K1, D_MODEL)


def _seq_dft_latent(pq, proj, tables, y_prev):
    bsz = proj.shape[0]
    side = GRID_W
    cs_a, cs_b, tw_c, tw_s = tables
    n_rows_all = N_TOK // side
    full = lambda shape: pl.BlockSpec(shape, lambda b, t: (0,) * len(shape))
    z = pl.pallas_call(
        _row_dft_kernel,
        grid=(bsz, side // FFT_COLS),
        in_specs=[pl.BlockSpec((1, 2, side, FFT_COLS * D_MODEL), lambda b, t: (b, 0, 0, t)),
                  full((2, side, side)),
                  pl.BlockSpec((FFT_COLS, side, LANES), lambda b, t: (t, 0, 0)),
                  pl.BlockSpec((FFT_COLS, side, LANES), lambda b, t: (t, 0, 0))],
        out_specs=pl.BlockSpec((1, 2, side, FFT_COLS * D_MODEL), lambda b, t: (b, 0, 0, t)),
        out_shape=jax.ShapeDtypeStruct((bsz, 2, side, side * D_MODEL), BF16),
        compiler_params=_cparams("parallel", "parallel"),
        name="fourier_row_dft",
    )(pq, cs_a, tw_c, tw_s)
    y = pl.pallas_call(
        _col_dft_kernel,
        grid=(bsz, side // FFT_K1),
        in_specs=[pl.BlockSpec((1, 2, FFT_K1, side, D_MODEL), lambda b, t: (b, 0, t, 0, 0)),
                  full((2, FFT_K1 * side, FFT_K1 * side)),
                  pl.BlockSpec((1, side, FFT_K1, D_MODEL), lambda b, t: (b, 0, t, COL_CZ)),
                  pl.BlockSpec(memory_space=pl.ANY)],
        out_specs=pl.BlockSpec((1, side, FFT_K1, D_MODEL), lambda b, t: (b, 0, t, 0)),
        out_shape=jax.ShapeDtypeStruct((bsz, n_rows_all, side, D_MODEL), BF16),
        input_output_aliases={3: 0},
        compiler_params=_cparams("parallel", "parallel"),
        name="fourier_col_dft",
    )(z.reshape(bsz, 2, side, side, D_MODEL), cs_b, proj.reshape(bsz, n_rows_all, side, N_PROJ),
      y_prev.reshape(bsz, n_rows_all, side, D_MODEL))
    return y.reshape(bsz, N_TOK, D_MODEL)


def _seq_dft_ctx_kernel(a_ref, pq_ref, z_ref, y_in_ref, y_ref):
    del y_in_ref
    r = _dot(a_ref[:, :N_CTX], pq_ref[0, 0]) + _dot(a_ref[:, N_CTX:], pq_ref[0, 1])
    y_ref[0] = (r * _silu(z_ref[0].astype(F32))).astype(y_ref.dtype)


def _seq_dft_ctx(dft_ctx, pq, proj, y_c):
    bsz = proj.shape[0]
    ctx_blk = N_LAT // N_CTX
    return pl.pallas_call(
        _seq_dft_ctx_kernel,
        grid=(bsz,),
        in_specs=[
            pl.BlockSpec((N_CTX, 2 * N_CTX), lambda b: (0, 0)),
            pl.BlockSpec((1, 2, N_CTX, D_MODEL), lambda b: (b, 0, 0, 0)),
            pl.BlockSpec((1, N_CTX, D_MODEL), lambda b: (b, ctx_blk, COL_CZ)),
            pl.BlockSpec(memory_space=pl.ANY),
        ],
        out_specs=pl.BlockSpec((1, N_CTX, D_MODEL), lambda b: (b, ctx_blk, 0)),
        out_shape=jax.ShapeDtypeStruct(y_c.shape, y_c.dtype),
        input_output_aliases={3: 0},
        compiler_params=_cparams("parallel"),
        name="fourier_token_dft_ctx",
    )(dft_ctx, pq, proj, y_c)


def _merge_kernel(x_ref, hf_ref, hb_ref, ao_ref, az_ref, yb_ref, yc_ref, ga_ref, gb_ref, gc_ref,
                  mod_ref, nw_ref, wa_ref, wb_ref, wc_ref, wo_ref, o_ref, ya_ref, *, tile):
    b = pl.program_id(0)

    for h in range(A_HEADS):
        cols = slice(h * A_V, (h + 1) * A_V)
        s = hf_ref[0, :, cols].astype(F32) + hb_ref[0, :, cols].astype(F32)
        sn = s * lax.rsqrt(jnp.mean(s * s, axis=-1, keepdims=True) + EPS) * nw_ref[:, cols]
        gate = _sigmoid(ao_ref[0, :, cols].astype(F32)) * _silu(az_ref[0, :, cols].astype(F32))
        ya_ref[:, cols] = (sn * gate).astype(BF16)

    def branch(y, g_ref, w_ref):
        return _sigmoid(g_ref[0].astype(F32)) * _dot(y, w_ref[...])

    y = (branch(ya_ref[...], ga_ref, wa_ref) + branch(yb_ref[0], gb_ref, wb_ref)
         + branch(yc_ref[0], gc_ref, wc_ref))
    out = _dot(y.astype(BF16), wo_ref[...])
    gate_cols = slice(2 * D_MODEL, 3 * D_MODEL)
    gate = jnp.where(_ctx_rows(pl.program_id(1), tile),
                     mod_ref[pl.ds(B_CTX_ROW, 1), gate_cols], mod_ref[pl.ds(b, 1), gate_cols])
    o_ref[0] = x_ref[0] + gate * out


def _merge(xs, h_fwd, h_bwd, y_b, y_c, proj, mod_l, a_norm_w, wa, wb, wc, wo, n_rows, tile):
    bsz = xs.shape[0]
    col = lambda c: pl.BlockSpec((1, tile, D_MODEL), lambda b, t: (b, t, c))
    weight = pl.BlockSpec((D_MODEL, D_MODEL), lambda b, t: (0, 0))
    return pl.pallas_call(
        functools.partial(_merge_kernel, tile=tile),
        grid=(bsz, n_rows // tile),
        in_specs=[col(0), col(0), col(0), col(COL_AO), col(COL_AZ), col(0), col(0),
                  col(COL_MG), col(COL_MG + 1), col(COL_MG + 2),
                  pl.BlockSpec((SUBLANES, 3 * D_MODEL), lambda b, t: (0, 0)),
                  pl.BlockSpec((1, D_MODEL), lambda b, t: (0, 0)),
                  weight, weight, weight, weight],
        out_specs=col(0),
        out_shape=jax.ShapeDtypeStruct((bsz, n_rows, D_MODEL), F32),
        scratch_shapes=[pltpu.VMEM((tile, D_MODEL), BF16)],
        compiler_params=_cparams("parallel", "parallel"),
        name="merge_out_residual",
    )(xs, h_fwd, h_bwd, proj, proj, y_b, y_c, proj, proj, proj, mod_l, a_norm_w, wa, wb, wc, wo)


def _rope_tables():
    n = jnp.arange(N_LAT, dtype=jnp.int32)
    row = (n // GRID_W).astype(F32)
    col = (n % GRID_W).astype(F32)
    inv_freq = ROPE_THETA ** (-jnp.arange(0, ROPE_AXIS_DIM, 2, dtype=F32) / ROPE_AXIS_DIM)
    ang_r = row[:, None] * inv_freq
    ang_c = col[:, None] * inv_freq
    cos = jnp.concatenate([jnp.cos(ang_r)] * 2 + [jnp.cos(ang_c)] * 2, axis=-1)
    sin = jnp.concatenate([-jnp.sin(ang_r), jnp.sin(ang_r), -jnp.sin(ang_c), jnp.sin(ang_c)], axis=-1)
    cos = jnp.concatenate([cos, jnp.ones((N_CTX, B_QK), F32)], axis=0)
    sin = jnp.concatenate([sin, jnp.zeros((N_CTX, B_QK), F32)], axis=0)
    reps = LANES // B_QK
    return jnp.tile(cos, (1, reps)), jnp.tile(sin, (1, reps))


def _dft_cos_sin(n, scale):
    idx = jnp.arange(n, dtype=jnp.int32)
    ang = ((idx[:, None] * idx[None, :]) % n).astype(F32) * (2.0 * math.pi / n)
    return jnp.cos(ang) * scale, jnp.sin(ang) * scale


def _dft_tables():
    c_ch, s_ch = _dft_cos_sin(C_GROUP_DIM, 1.0)
    chan = jnp.concatenate([c_ch, s_ch], axis=1).astype(BF16)
    c_c, s_c = _dft_cos_sin(N_CTX, (N_CTX * C_GROUP_DIM) ** -0.5)
    ctx = jnp.concatenate([c_c, -s_c], axis=1).astype(BF16)
    c_s, s_s = _dft_cos_sin(GRID_W, 1.0)
    stage_a = jnp.stack([c_s, s_s]).astype(BF16)
    eye = jnp.eye(FFT_K1, dtype=F32)
    expand = lambda t: jnp.einsum("kn,ij->kijn", t, eye).reshape(
        FFT_K1 * GRID_W, FFT_K1 * GRID_W)
    stage_b = (jnp.stack([expand(c_s), expand(s_s)]) * (N_LAT * C_GROUP_DIM) ** -0.5).astype(BF16)
    idx = jnp.arange(GRID_W, dtype=jnp.int32)
    ang = (idx[:, None] * idx[None, :]).astype(F32) * (2.0 * math.pi / N_LAT)
    lanes = lambda t: jnp.broadcast_to(t[:, :, None], (GRID_W, GRID_W, LANES))
    return chan, ctx, (stage_a, stage_b, lanes(jnp.cos(ang)), lanes(jnp.sin(ang)))


def _group_mean_matrix():
    g = np.arange(LANES) // B_QK
    return jnp.asarray((g[:, None] == g[None, :]) / B_QK, dtype=BF16)


def kernel(x, c, ctx, c_ctx, norm_w, w_ada, b_ada, w_in, b_if, a_norm_w, q_norm_w, k_norm_w,
           lambda_q1, lambda_k1, lambda_q2, lambda_k2, subln_w, w_a_out, w_b_out, w_c_out, w_out):
    bsz = x.shape[0]
    xs = jnp.concatenate([x, ctx], axis=1)
    c_rows = jnp.concatenate(
        [c, c_ctx[None, :], jnp.zeros((SUBLANES - bsz - 1, D_MODEL), F32)], axis=0)
    mod, lam = _modulation(c_rows, w_ada, b_ada, lambda_q1, lambda_k1, lambda_q2, lambda_k2)

    cos_t, sin_t = _rope_tables()
    dft_chan, dft_ctx, dft_lat = _dft_tables()
    group_mean = _group_mean_matrix()

    w_in_t = jnp.swapaxes(w_in, 1, 2)
    w_tail_t = w_in_t[:, w_in_t.shape[1] - N_GATE:, :]
    lane_pad = LANES - N_GATE
    pad_lanes = lambda w: jnp.pad(w, ((0, 0),) * (w.ndim - 1) + ((0, lane_pad),))
    w_gate = pad_lanes(w_in[:, :, GATE_COL0:GATE_COL0 + N_GATE])
    b_gate = pad_lanes(b_if)
    wa, wb, wc, wo = (w.astype(BF16) for w in (w_a_out, w_b_out, w_c_out, w_out))
    tile2 = lambda w: jnp.tile(w[None, :], (1, LANES // B_QK))

    for l in range(DEPTH):
        last = l == DEPTH - 1
        h, gates = _norm_mod(xs, norm_w[l][None, :], mod[l], w_gate[l], b_gate[l][None, :])
        proj = _in_projection(h.reshape(bsz * N_TOK, D_MODEL), w_in_t, w_tail_t, l).reshape(bsz, N_TOK, N_PROJ)

        gates_t = jnp.swapaxes(gates[:, :, :N_GATE], 1, 2)
        h_fwd, h_bwd = _mlstm(proj, gates, gates_t)

        qh, kt = _qk_prep(proj, cos_t, sin_t, tile2(q_norm_w[l]), tile2(k_norm_w[l]), group_mean)
        lam_l = lam[l, 0, :1]
        lam_init = 0.8 - 0.6 * math.exp(-0.3 * l)
        y_b = _diff_attention(lam_l, qh, kt, proj, subln_w[l][None, :], lam_init, h, ctx=False)
        if not last:
            y_b = _diff_attention(lam_l, qh, kt, proj, subln_w[l][None, :], lam_init, y_b, ctx=True)

        pq, pq_ctx = _chan_dft(proj, dft_chan)
        y_c = _seq_dft_latent(pq, proj, dft_lat, qh)

        if last:
            return _merge(xs, h_fwd, h_bwd, y_b, y_c, proj, mod[l], a_norm_w[l][None, :],
                          wa[l], wb[l], wc[l], wo[l], n_rows=N_LAT, tile=LAST_TILE)
        y_c = _seq_dft_ctx(dft_ctx, pq_ctx, proj, y_c)
        xs = _merge(xs, h_fwd, h_bwd, y_b, y_c, proj, mod[l], a_norm_w[l][None, :],
                    wa[l], wb[l], wc[l], wo[l], n_rows=N_TOK, tile=MERGE_TILE)
```
